```python
import jax, jax.numpy as jnp
from jax import lax
import numpy as np

D_MODEL = 2048
BATCH = 4
SEQ = 2048
DEPTH = 4
DEC_BATCH = 8
DEC_SEQ = 8
PAST_LEN = 16384
PAGE_SIZE = 128

D_MIX = D_MODEL
FOX_HEADS = 8
FOX_HEAD_DIM = 128
FOX_WIDTH = FOX_HEADS * FOX_HEAD_DIM
GLA_HEADS = 4
GLA_WIDTH = D_MIX - FOX_WIDTH
GLA_DV = GLA_WIDTH // GLA_HEADS
GLA_KEY_WIDTH = GLA_WIDTH // 2
GLA_DK = GLA_KEY_WIDTH // GLA_HEADS
GLA_GATE_RANK = 16
GLA_GATE_NORMALIZER = 16.0
GLA_CHUNK = 64
Q_BLOCK = 128
D_FF = 5632
CONV_WIDTH = 3
EPS = 1e-6
IN_SIZES = (FOX_WIDTH, FOX_WIDTH, FOX_WIDTH, FOX_HEADS,
            GLA_KEY_WIDTH, GLA_KEY_WIDTH, GLA_WIDTH, GLA_WIDTH, GLA_GATE_RANK)
N_IN = 3 * FOX_WIDTH + FOX_HEADS + 2 * GLA_KEY_WIDTH + 2 * GLA_WIDTH + GLA_GATE_RANK

kernel_name = "hymba_fox_gla_convffn_step"


def rmsnorm(x, g):
    xf = x.astype(jnp.float32)
    y = xf * lax.rsqrt(jnp.mean(xf * xf, axis=-1, keepdims=True) + EPS)
    return (y * g.astype(jnp.float32)).astype(x.dtype)


def project(xn, w_in_l, b_f_l, w_a2_l, b_a_l):
    B, L = xn.shape[:2]
    idx = np.cumsum(IN_SIZES)[:-1].tolist()
    fq, fk, fv, fl, gq, gk, gv, gr, ga = jnp.split(xn @ w_in_l, idx, axis=-1)
    logf = jax.nn.log_sigmoid((fl + b_f_l).astype(jnp.float32))
    la = jax.nn.log_sigmoid((ga @ w_a2_l + b_a_l).astype(jnp.float32)) / GLA_GATE_NORMALIZER
    fh = lambda t: t.reshape(B, L, FOX_HEADS, FOX_HEAD_DIM)
    gh = lambda t: t.reshape(B, L, GLA_HEADS, -1)
    return fh(fq), fh(fk), fh(fv), logf, gh(gq), gh(gk), gh(gv), gr, gh(la)


def fox_prompt(q, k, v, logf):
    B, S, H, D = q.shape
    qb_len = min(Q_BLOCK, S)
    n_blk = S // qb_len
    scale = D ** -0.5
    c = jnp.cumsum(logf, axis=1)
    c_k = c.transpose(0, 2, 1)
    q_blocks = q.reshape(B, n_blk, qb_len, H, D).transpose(1, 0, 2, 3, 4)
    c_blocks = c.reshape(B, n_blk, qb_len, H).transpose(1, 0, 3, 2)
    starts = jnp.arange(n_blk, dtype=jnp.int32) * qb_len
    key_pos = jnp.arange(S, dtype=jnp.int32)

    def block(args):
        q_i, cq_i, i0 = args
        s = jnp.einsum('bqhd,bkhd->bhqk', q_i, k).astype(jnp.float32) * scale
        s = s + (cq_i[..., :, None] - c_k[:, :, None, :])
        mask = (i0 + jnp.arange(qb_len, dtype=jnp.int32))[:, None] >= key_pos[None, :]
        p = jax.nn.softmax(jnp.where(mask, s, -jnp.inf), axis=-1)
        return jnp.einsum('bhqk,bkhd->bqhd', p.astype(v.dtype), v)

    o = lax.map(block, (q_blocks, c_blocks, starts))
    return o.transpose(1, 0, 2, 3, 4).reshape(B, S, H, D)


def fox_sample(q, k_new, v_new, logf_new, k_past, v_past, logf_past):
    T = q.shape[1]
    P = k_past.shape[1]
    scale = FOX_HEAD_DIM ** -0.5
    c_new = jnp.cumsum(logf_new, axis=1).transpose(0, 2, 1)
    lf_p = logf_past.astype(jnp.float32)
    c_past = -(lax.cumsum(lf_p, axis=1, reverse=True) - lf_p).transpose(0, 2, 1)
    s_past = jnp.einsum('bqhd,bkhd->bhqk', q, k_past).astype(jnp.float32) * scale
    s_past = s_past + (c_new[..., :, None] - c_past[:, :, None, :])
    s_new = jnp.einsum('bqhd,bkhd->bhqk', q, k_new).astype(jnp.float32) * scale
    s_new = s_new + (c_new[..., :, None] - c_new[:, :, None, :])
    causal = jnp.tril(jnp.ones((T, T), dtype=bool))
    s_new = jnp.where(causal, s_new, -jnp.inf)
    p = jax.nn.softmax(jnp.concatenate([s_past, s_new], axis=-1), axis=-1).astype(v_new.dtype)
    return (jnp.einsum('bhqk,bkhd->bqhd', p[..., :P], v_past)
            + jnp.einsum('bhqk,bkhd->bqhd', p[..., P:], v_new))


def gla_chunked(q, k, v, log_alpha, s0):
    B, L, H, _ = q.shape
    c = GLA_CHUNK if L % GLA_CHUNK == 0 else L
    n = L // c
    out_dtype = v.dtype

    def to_chunks(t):
        return t.astype(jnp.float32).reshape(B, n, c, H, t.shape[-1]).transpose(1, 0, 3, 2, 4)

    qs = to_chunks(q) * (GLA_DK ** -0.5)
    ks, vs, gs = to_chunks(k), to_chunks(v), to_chunks(log_alpha)
    tri = jnp.tril(jnp.ones((c, c), dtype=bool))

    def step(S, inp):
        qc, kc, vc, gc = inp
        b = jnp.cumsum(gc, axis=2)
        o_inter = jnp.einsum('bhtk,bhkv->bhtv', qc * jnp.exp(b), S)
        diff = b[:, :, :, None, :] - b[:, :, None, :, :]
        decay = jnp.exp(jnp.where(tri[:, :, None], diff, -jnp.inf))
        A = jnp.einsum('bhtk,bhsk,bhtsk->bhts', qc, kc, decay)
        o_intra = jnp.einsum('bhts,bhsv->bhtv', A, vc)
        b_last = b[:, :, -1:, :]
        S_new = (jnp.exp(b_last[:, :, 0, :])[..., None] * S
                 + jnp.einsum('bhsk,bhsv->bhkv', kc * jnp.exp(b_last - b), vc))
        return S_new, o_inter + o_intra

    S_fin, o = lax.scan(step, s0.astype(jnp.float32), (qs, ks, vs, gs))
    o = o.transpose(1, 0, 3, 2, 4).reshape(B, L, H, -1)
    return o.astype(out_dtype), S_fin


def merge_heads(fo, go, gr, fox_norm_l, gla_norm_l, w_out_l):
    B, L = fo.shape[:2]
    fo = rmsnorm(fo, fox_norm_l.reshape(FOX_HEADS, FOX_HEAD_DIM)).reshape(B, L, FOX_WIDTH)
    go = rmsnorm(go, gla_norm_l.reshape(GLA_HEADS, GLA_DV)).reshape(B, L, GLA_WIDTH)
    go = go * jax.nn.silu(gr)
    return jnp.concatenate([fo, go], axis=-1) @ w_out_l


def conv_ffn(xn, buf, w_gate_l, w_up_l, conv_w_l, conv_b_l, w_down_l):
    L = xn.shape[1]
    g = xn @ w_gate_l
    u = xn @ w_up_l
    g_ext = jnp.concatenate([buf.astype(g.dtype), g], axis=1)
    conv = conv_b_l
    for i in range(CONV_WIDTH):
        conv = conv + conv_w_l[i] * g_ext[:, i:i + L]
    h = jax.nn.silu(conv) * u
    return h @ w_down_l, g_ext[:, L:]


def setup_inputs(seed: int = 0) -> dict:
    key = jax.random.key(seed)
    ks = jax.random.split(key, 24)
    f32 = jnp.float32
    n_pages = PAST_LEN // PAGE_SIZE
    n_used = DEC_BATCH * n_pages
    n_pool = n_used + n_used // 4
    nrm = lambda k, shape, s: jax.random.normal(k, shape, f32) * s
    page_table = jax.random.permutation(ks[7], n_pool)[:n_used].reshape(DEC_BATCH, n_pages).astype(jnp.int32)
    return {
        "x_prompt": nrm(ks[0], (BATCH, SEQ, D_MODEL), 1.0),
        "x_sample": nrm(ks[1], (DEC_BATCH, DEC_SEQ, D_MODEL), 1.0),
        "cache_k": nrm(ks[2], (DEPTH, n_pool, PAGE_SIZE, FOX_HEADS, FOX_HEAD_DIM), 1.0),
        "cache_v": nrm(ks[3], (DEPTH, n_pool, PAGE_SIZE, FOX_HEADS, FOX_HEAD_DIM), 1.0),
        "cache_logf": jax.nn.log_sigmoid(3.0 + nrm(ks[4], (DEPTH, n_pool, PAGE_SIZE, FOX_HEADS), 1.0)),
        "state_gla": nrm(ks[5], (DEPTH, DEC_BATCH, GLA_HEADS, GLA_DK, GLA_DV), 0.5),
        "state_conv": nrm(ks[6], (DEPTH, DEC_BATCH, CONV_WIDTH - 1, D_FF), 1.0),
        "page_table": page_table,
        "w_in": nrm(ks[8], (DEPTH, D_MODEL, N_IN), D_MODEL ** -0.5),
        "b_f": 3.0 + jax.random.uniform(ks[9], (DEPTH, FOX_HEADS), f32, -1.0, 1.0),
        "w_a2": nrm(ks[10], (DEPTH, GLA_GATE_RANK, GLA_KEY_WIDTH), GLA_GATE_RANK ** -0.5),
        "b_a": nrm(ks[11], (DEPTH, GLA_KEY_WIDTH), 0.1),
        "fox_norm": 1.0 + nrm(ks[12], (DEPTH, FOX_WIDTH), 0.02),
        "gla_norm": 1.0 + nrm(ks[13], (DEPTH, GLA_WIDTH), 0.02),
        "w_out": nrm(ks[14], (DEPTH, D_MIX, D_MODEL), D_MIX ** -0.5),
        "norm_attn": 1.0 + nrm(ks[15], (DEPTH, D_MODEL), 0.02),
        "norm_ffn": 1.0 + nrm(ks[16], (DEPTH, D_MODEL), 0.02),
        "w_gate": nrm(ks[17], (DEPTH, D_MODEL, D_FF), D_MODEL ** -0.5),
        "w_up": nrm(ks[18], (DEPTH, D_MODEL, D_FF), D_MODEL ** -0.5),
        "conv_w": nrm(ks[19], (DEPTH, CONV_WIDTH, D_FF), CONV_WIDTH ** -0.5),
        "conv_b": nrm(ks[20], (DEPTH, D_FF), 0.02),
        "w_down": nrm(ks[21], (DEPTH, D_FF, D_MODEL), D_FF ** -0.5),
        "norm_final": 1.0 + nrm(ks[22], (D_MODEL,), 0.02),
    }


def reference(x_prompt, x_sample, cache_k, cache_v, cache_logf, state_gla, state_conv, page_table,
              w_in, b_f, w_a2, b_a, fox_norm, gla_norm, w_out, norm_attn, norm_ffn,
              w_gate, w_up, conv_w, conv_b, w_down, norm_final):
    db, n_pages = page_table.shape
    past_len = n_pages * PAGE_SIZE
    bp = x_prompt.shape[0]
    xp, xs = x_prompt, x_sample
    kp, vp, lp, sp, cp = [], [], [], [], []
    ksm, vsm, lsm, ssm, csm = [], [], [], [], []
    for l in range(DEPTH):
        xn = rmsnorm(xp, norm_attn[l])
        fq, fk, fv, logf, gq, gk, gv, gr, la = project(xn, w_in[l], b_f[l], w_a2[l], b_a[l])
        fo = fox_prompt(fq, fk, fv, logf)
        s0 = jnp.zeros((bp, GLA_HEADS, GLA_DK, GLA_DV), jnp.float32)
        go, s_fin = gla_chunked(gq, gk, gv, la, s0)
        xp = xp + merge_heads(fo, go, gr, fox_norm[l], gla_norm[l], w_out[l])
        buf0 = jnp.zeros((bp, CONV_WIDTH - 1, D_FF), xp.dtype)
        y, buf = conv_ffn(rmsnorm(xp, norm_ffn[l]), buf0, w_gate[l], w_up[l], conv_w[l], conv_b[l], w_down[l])
        xp = xp + y
        kp.append(fk.astype(cache_k.dtype)); vp.append(fv.astype(cache_v.dtype))
        lp.append(logf.astype(cache_logf.dtype)); sp.append(s_fin.astype(state_gla.dtype))
        cp.append(buf.astype(state_conv.dtype))
        xn = rmsnorm(xs, norm_attn[l])
        fq, fk, fv, logf, gq, gk, gv, gr, la = project(xn, w_in[l], b_f[l], w_a2[l], b_a[l])
        k_past = cache_k[l][page_table].reshape(db, past_len, FOX_HEADS, FOX_HEAD_DIM)
        v_past = cache_v[l][page_table].reshape(db, past_len, FOX_HEADS, FOX_HEAD_DIM)
        lf_past = cache_logf[l][page_table].reshape(db, past_len, FOX_HEADS)
        fo = fox_sample(fq, fk, fv, logf, k_past, v_past, lf_past)
        go, s_fin = gla_chunked(gq, gk, gv, la, state_gla[l])
        xs = xs + merge_heads(fo, go, gr, fox_norm[l], gla_norm[l], w_out[l])
        y, buf = conv_ffn(rmsnorm(xs, norm_ffn[l]), state_conv[l], w_gate[l], w_up[l], conv_w[l], conv_b[l], w_down[l])
        xs = xs + y
        ksm.append(fk.astype(cache_k.dtype)); vsm.append(fv.astype(cache_v.dtype))
        lsm.append(logf.astype(cache_logf.dtype)); ssm.append(s_fin.astype(state_gla.dtype))
        csm.append(buf.astype(state_conv.dtype))
    y_prompt = rmsnorm(xp, norm_final)
    y_sample = rmsnorm(xs, norm_final)
    return (y_prompt, y_sample,
            jnp.stack(kp), jnp.stack(vp), jnp.stack(lp), jnp.stack(sp), jnp.stack(cp),
            jnp.stack(ksm), jnp.stack(vsm), jnp.stack(lsm), jnp.stack(ssm), jnp.stack(csm))
```

```python
import functools

import jax
import jax.numpy as jnp
from jax import lax
from jax.experimental import pallas as pl
from jax.experimental.pallas import tpu as pltpu

F32 = jnp.float32
BF16 = jnp.bfloat16

D_MODEL = 2048
DEPTH = 4
PAGE = 128
FOX_HEADS = 8
FOX_DIM = 128
FOX_WIDTH = FOX_HEADS * FOX_DIM
GLA_HEADS = 4
GLA_DK = 128
GLA_DV = 256
GLA_KEY_WIDTH = GLA_HEADS * GLA_DK
GLA_WIDTH = GLA_HEADS * GLA_DV
GLA_RANK = 16
GLA_NORMALIZER = 16.0
D_FF = 5632
CONV_W = 3
EPS = 1e-6

LANES = 128
SUBLANES = 8
BF16_ROWS = 16
VMEM_LIMIT = 56 * 1024 * 1024

GLA_CHUNK = 128
GLA_SUB = 16
NEG_BIG = -1e30

P1_FQ, P1_GQ, P1_GK, P1_GV, P1_GR = 0, 1024, 1536, 2048, 3072
P1_WIDTH = 4096


def _params(sem):
    return pltpu.CompilerParams(dimension_semantics=sem, vmem_limit_bytes=VMEM_LIMIT)


def _log_sigmoid(x):
    return jnp.minimum(x, 0.0) - jnp.log1p(jnp.exp(-jnp.abs(x)))


def _silu(x):
    return x * (1.0 / (1.0 + jnp.exp(-x)))


def _rms(x, g):
    return x * lax.rsqrt(jnp.mean(x * x, axis=-1, keepdims=True) + EPS) * g


def _dot(a, b):
    return jnp.dot(a, b, preferred_element_type=F32)


def _dot_nt(a, b):
    return lax.dot_general(a, b, (((1,), (1,)), ((), ())), preferred_element_type=F32)


def _dot_tn(a, b):
    return lax.dot_general(a, b, (((0,), (0,)), ((), ())), preferred_element_type=F32)


def _split3(x):
    hi = x.astype(BF16)
    r1 = x - hi.astype(F32)
    mid = r1.astype(BF16)
    lo = (r1 - mid.astype(F32)).astype(BF16)
    return hi, mid, lo


def _norm_small_kernel(x_ref, g_ref, ws_ref, bs_ref, wa2_ref, ba_ref, wft_ref, bft_ref,
                       xn_ref, logf_ref, la_ref, c_ref, carry_ref, *, tm, seg):
    i = pl.program_id(0)
    xn = _rms(x_ref[...], g_ref[...]).astype(BF16)
    xn_ref[...] = xn
    zs = _dot(xn, ws_ref[...])
    logf_ref[...] = _log_sigmoid(zs + bs_ref[...])[:, :FOX_HEADS]
    la_pre = _dot(zs.astype(BF16), wa2_ref[...]) + ba_ref[...]
    la_ref[...] = _log_sigmoid(la_pre) * (1.0 / GLA_NORMALIZER)

    lft = _log_sigmoid(_dot_nt(wft_ref[...], xn) + bft_ref[...])[:FOX_HEADS]
    lane = lax.broadcasted_iota(jnp.int32, (FOX_HEADS, tm), 1)
    pos = lane & (seg - 1) if seg < tm else lane
    c = lft
    d = 1
    while d < min(seg, tm):
        c = c + jnp.where(pos >= d, pltpu.roll(c, d, axis=1), 0.0)
        d *= 2
    if seg > tm:
        tiles_per_seq = seg // tm

        @pl.when(i % tiles_per_seq == 0)
        def _():
            carry_ref[...] = jnp.zeros_like(carry_ref)

        c = c + carry_ref[:, 0:1]
        carry_ref[...] = jnp.broadcast_to(c[:, tm - 1:tm], carry_ref.shape)
    c_ref[0] = c


def norm_small(x, gamma, w_small, b_small, wa2_pad, b_a, wf_t, bf_t, *, tm, seg):
    m = x.shape[0]
    n_tiles = m // tm
    if seg >= tm:
        tps = seg // tm
        c_shape = (m // seg, FOX_HEADS, seg)
        c_map = lambda i: (i // tps, 0, i % tps)
    else:
        c_shape = (n_tiles, FOX_HEADS, tm)
        c_map = lambda i: (i, 0, 0)
    full = lambda i: (0, 0)
    return pl.pallas_call(
        functools.partial(_norm_small_kernel, tm=tm, seg=seg),
        grid=(n_tiles,),
        in_specs=[
            pl.BlockSpec((tm, D_MODEL), lambda i: (i, 0)),
            pl.BlockSpec((1, D_MODEL), full),
            pl.BlockSpec((D_MODEL, LANES), full),
            pl.BlockSpec((1, LANES), full),
            pl.BlockSpec((LANES, GLA_KEY_WIDTH), full),
            pl.BlockSpec((1, GLA_KEY_WIDTH), full),
            pl.BlockSpec((BF16_ROWS, D_MODEL), full),
            pl.BlockSpec((BF16_ROWS, 1), full),
        ],
        out_specs=[
            pl.BlockSpec((tm, D_MODEL), lambda i: (i, 0)),
            pl.BlockSpec((tm, FOX_HEADS), lambda i: (i, 0)),
            pl.BlockSpec((tm, GLA_KEY_WIDTH), lambda i: (i, 0)),
            pl.BlockSpec((1, FOX_HEADS, tm), c_map),
        ],
        out_shape=[
            jax.ShapeDtypeStruct((m, D_MODEL), BF16),
            jax.ShapeDtypeStruct((m, FOX_HEADS), F32),
            jax.ShapeDtypeStruct((m, GLA_KEY_WIDTH), F32),
            jax.ShapeDtypeStruct(c_shape, F32),
        ],
        scratch_shapes=[pltpu.VMEM((FOX_HEADS, LANES), F32)],
        compiler_params=_params(("arbitrary",)),
        name="norm_small",
    )(x, gamma, w_small, b_small, wa2_pad, b_a, wf_t, bf_t)


def _mm_kernel(*refs, n_a, has_scale, has_res, n_out):
    a_refs = refs[:n_a]
    w_refs = refs[n_a:2 * n_a]
    pos = 2 * n_a
    acc = _dot(a_refs[0][...], w_refs[0][...])
    for a_ref, w_ref in zip(a_refs[1:], w_refs[1:]):
        acc = acc + _dot(a_ref[...], w_ref[...])
    if has_scale:
        acc = acc * refs[pos][...]
        pos += 1
    if has_res:
        acc = refs[pos][...] + acc
        pos += 1
    for o_ref in refs[pos:pos + n_out]:
        o_ref[...] = acc.astype(o_ref.dtype)


def matmul(a_list, w_list, *, out_dtypes, tm, tn, scale=None, res=None):
    m = a_list[0].shape[0]
    n = w_list[0].shape[1]
    in_specs = [pl.BlockSpec((tm, a.shape[1]), lambda i, j: (i, 0)) for a in a_list]
    in_specs += [pl.BlockSpec((w.shape[0], tn), lambda i, j: (0, j)) for w in w_list]
    args = list(a_list) + list(w_list)
    if scale is not None:
        in_specs.append(pl.BlockSpec((1, tn), lambda i, j: (0, j)))
        args.append(scale)
    if res is not None:
        in_specs.append(pl.BlockSpec((tm, tn), lambda i, j: (i, j)))
        args.append(res)
    outs = pl.pallas_call(
        functools.partial(_mm_kernel, n_a=len(a_list), has_scale=scale is not None,
                          has_res=res is not None, n_out=len(out_dtypes)),
        grid=(m // tm, n // tn),
        in_specs=in_specs,
        out_specs=[pl.BlockSpec((tm, tn), lambda i, j: (i, j)) for _ in out_dtypes],
        out_shape=[jax.ShapeDtypeStruct((m, n), dt) for dt in out_dtypes],
        compiler_params=_params(("parallel", "arbitrary")),
        name="matmul",
    )(*args)
    return outs


def _fox_prompt_kernel(q_ref, k_ref, v_ref, ck_ref, gn_ref, o_ref, m_sc, l_sc, acc_sc, *, tq):
    qi = pl.program_id(2)
    ki = pl.program_id(3)

    @pl.when(ki == 0)
    def _():
        m_sc[...] = jnp.full_like(m_sc, -jnp.inf)
        l_sc[...] = jnp.zeros_like(l_sc)
        acc_sc[...] = jnp.zeros_like(acc_sc)

    def step(masked):
        s = _dot_nt(q_ref[...], k_ref[...]) - ck_ref[0]
        if masked:
            row = lax.broadcasted_iota(jnp.int32, s.shape, 0)
            col = lax.broadcasted_iota(jnp.int32, s.shape, 1)
            s = jnp.where(row >= col, s, -jnp.inf)
        m_old = m_sc[...]
        m_new = jnp.maximum(m_old, jnp.max(s, axis=-1, keepdims=True))
        alpha = jnp.exp(m_old - m_new)
        p = jnp.exp(s - m_new)
        l_sc[...] = alpha * l_sc[...] + jnp.sum(p, axis=-1, keepdims=True)
        acc_sc[...] = alpha * acc_sc[...] + _dot(p.astype(BF16), v_ref[...])
        m_sc[...] = m_new

    @pl.when(ki < qi)
    def _():
        step(False)

    @pl.when(ki == qi)
    def _():
        step(True)
        o = acc_sc[...] / l_sc[...]
        o_ref[...] = _rms(o, gn_ref[...]).astype(o_ref.dtype)


def fox_prompt(p1, kv, c, fox_norm, *, batch, seq, tq):
    m = batch * seq
    nq = seq // tq
    c3 = c.reshape(batch * FOX_HEADS, 1, seq)
    kv_row = lambda b, h, qi, ki: b * nq + jnp.minimum(ki, qi)
    return pl.pallas_call(
        functools.partial(_fox_prompt_kernel, tq=tq),
        grid=(batch, FOX_HEADS, nq, nq),
        in_specs=[
            pl.BlockSpec((tq, FOX_DIM), lambda b, h, qi, ki: (b * nq + qi, h)),
            pl.BlockSpec((tq, FOX_DIM), lambda b, h, qi, ki: (kv_row(b, h, qi, ki), h)),
            pl.BlockSpec((tq, FOX_DIM), lambda b, h, qi, ki: (kv_row(b, h, qi, ki), FOX_HEADS + h)),
            pl.BlockSpec((1, 1, tq), lambda b, h, qi, ki: (b * FOX_HEADS + h, 0, jnp.minimum(ki, qi))),
            pl.BlockSpec((1, FOX_DIM), lambda b, h, qi, ki: (0, h)),
        ],
        out_specs=pl.BlockSpec((tq, FOX_DIM), lambda b, h, qi, ki: (b * nq + qi, h)),
        out_shape=jax.ShapeDtypeStruct((m, FOX_WIDTH), BF16),
        scratch_shapes=[pltpu.VMEM((tq, 1), F32), pltpu.VMEM((tq, 1), F32),
                        pltpu.VMEM((tq, FOX_DIM), F32)],
        compiler_params=_params(("parallel", "parallel", "parallel", "arbitrary")),
        name="fox_prompt",
    )(p1, kv, kv, c3, fox_norm)


def _fox_sample_kernel(pt_ref, q_ref, kn_ref, vn_ref, cn_ref, gn_ref, *rest, pages, t_new):
    k_refs = rest[:pages]
    v_refs = rest[pages:2 * pages]
    lf_refs = rest[2 * pages:3 * pages]
    o_ref, m_sc, l_sc, acc_sc, carry_sc = rest[3 * pages:]
    del pt_ref
    step = pl.program_id(1)
    n_steps = pl.num_programs(1)
    rows = t_new

    @pl.when(step == 0)
    def _():
        m_sc[...] = jnp.full_like(m_sc, NEG_BIG)
        l_sc[...] = jnp.zeros_like(l_sc)
        acc_sc[...] = jnp.zeros_like(acc_sc)
        carry_sc[...] = jnp.zeros_like(carry_sc)

    lane = lax.broadcasted_iota(jnp.int32, (FOX_HEADS, PAGE), 1)
    bias = [None] * pages
    carry = carry_sc[:, 0:1]
    for p in range(pages - 1, -1, -1):
        lf = lf_refs[p][0, 0]
        suf = lf
        d = 1
        while d < PAGE:
            suf = suf + jnp.where(lane < PAGE - d, pltpu.roll(suf, PAGE - d, axis=1), 0.0)
            d *= 2
        bias[p] = (suf - lf) + carry
        carry = carry + suf[:, 0:1]
    carry_sc[...] = jnp.broadcast_to(carry, carry_sc.shape)

    def online_update(h, s_list, v_list):
        sl = slice(h * rows, (h + 1) * rows)
        m_old = m_sc[sl, :]
        m_new = m_old
        for s in s_list:
            m_new = jnp.maximum(m_new, jnp.max(s, axis=-1, keepdims=True))
        alpha = jnp.exp(m_old - m_new)
        l_new = alpha * l_sc[sl, :]
        acc = alpha * acc_sc[sl, :]
        for s, v in zip(s_list, v_list):
            pr = jnp.exp(s - m_new)
            l_new = l_new + jnp.sum(pr, axis=-1, keepdims=True)
            acc = acc + _dot(pr.astype(BF16), v)
        m_sc[sl, :] = m_new
        l_sc[sl, :] = l_new
        acc_sc[sl, :] = acc

    for h in range(FOX_HEADS):
        q_h = q_ref[:, h * FOX_DIM:(h + 1) * FOX_DIM].astype(BF16)
        s_list, v_list = [], []
        for p in range(pages):
            k_h = k_refs[p][0, 0, pl.ds(h, PAGE, stride=FOX_HEADS), :].astype(BF16)
            v_h = v_refs[p][0, 0, pl.ds(h, PAGE, stride=FOX_HEADS), :].astype(BF16)
            s_list.append(_dot_nt(q_h, k_h) + bias[p][h:h + 1, :])
            v_list.append(v_h)
        online_update(h, s_list, v_list)

    @pl.when(step == n_steps - 1)
    def _():
        row = lax.broadcasted_iota(jnp.int32, (rows, PAGE), 0)
        col = lax.broadcasted_iota(jnp.int32, (rows, PAGE), 1)
        valid = (col <= row) & (col < t_new)
        for h in range(FOX_HEADS):
            cols = slice(h * FOX_DIM, (h + 1) * FOX_DIM)
            s = _dot_nt(q_ref[:, cols].astype(BF16), kn_ref[:, cols]) - cn_ref[0, h:h + 1, :]
            s = jnp.where(valid, s, NEG_BIG)
            online_update(h, [s], [vn_ref[:, cols]])
            sl = slice(h * rows, (h + 1) * rows)
            o = acc_sc[sl, :] / l_sc[sl, :]
            o_ref[:, cols] = _rms(o, gn_ref[:, cols])


def fox_sample(page_table, q, k_new_pad, v_new_pad, c_new_pad, fox_norm, cache_k4, cache_v4,
               cache_lft, *, layer, pages):
    db, n_pages = page_table.shape
    t_new = q.shape[0] // db
    n_steps = n_pages // pages
    pt_flat = page_table.reshape(-1)

    def page_map(p):
        def index(b, s, pt):
            return (layer, pt[b * n_pages + (n_steps - 1 - s) * pages + p], 0, 0)
        return index

    in_specs = [
        pl.BlockSpec((t_new, FOX_WIDTH), lambda b, s, pt: (b, 0)),
        pl.BlockSpec((PAGE, FOX_WIDTH), lambda b, s, pt: (b, 0)),
        pl.BlockSpec((PAGE, FOX_WIDTH), lambda b, s, pt: (b, 0)),
        pl.BlockSpec((1, FOX_HEADS, PAGE), lambda b, s, pt: (b, 0, 0)),
        pl.BlockSpec((1, FOX_WIDTH), lambda b, s, pt: (0, 0)),
    ]
    in_specs += [pl.BlockSpec((1, 1, PAGE * FOX_HEADS, FOX_DIM), page_map(p)) for p in range(pages)]
    in_specs += [pl.BlockSpec((1, 1, PAGE * FOX_HEADS, FOX_DIM), page_map(p)) for p in range(pages)]
    in_specs += [pl.BlockSpec((1, 1, FOX_HEADS, PAGE), page_map(p)) for p in range(pages)]
    grid_spec = pltpu.PrefetchScalarGridSpec(
        num_scalar_prefetch=1,
        grid=(db, n_steps),
        in_specs=in_specs,
        out_specs=pl.BlockSpec((t_new, FOX_WIDTH), lambda b, s, pt: (b, 0)),
        scratch_shapes=[pltpu.VMEM((FOX_HEADS * t_new, 1), F32),
                        pltpu.VMEM((FOX_HEADS * t_new, 1), F32),
                        pltpu.VMEM((FOX_HEADS * t_new, FOX_DIM), F32),
                        pltpu.VMEM((FOX_HEADS, LANES), F32)],
    )
    return pl.pallas_call(
        functools.partial(_fox_sample_kernel, pages=pages, t_new=t_new),
        grid_spec=grid_spec,
        out_shape=jax.ShapeDtypeStruct((db * t_new, FOX_WIDTH), F32),
        compiler_params=_params(("parallel", "arbitrary")),
        name="fox_sample",
    )(pt_flat, q, k_new_pad, v_new_pad, c_new_pad, fox_norm,
      *([cache_k4] * pages), *([cache_v4] * pages), *([cache_lft] * pages))


def _gla_kernel(q_ref, k_ref, v_ref, la_ref, r_ref, gn_ref, s0_ref, o_ref, sfin_ref,
                s_sc, b_sc, *, chunk, sub):
    ci = pl.program_id(2)
    n_chunks = pl.num_programs(2)

    @pl.when(ci == 0)
    def _():
        s_sc[...] = s0_ref[0, 0]

    q = q_ref[...].astype(F32)
    k = k_ref[...].astype(F32)
    v = v_ref[...]
    rowi = lax.broadcasted_iota(jnp.int32, (chunk, chunk), 0)
    coli = lax.broadcasted_iota(jnp.int32, (chunk, chunk), 1)
    row = lax.broadcasted_iota(jnp.int32, (chunk, 1), 0)

    tri = jnp.where(rowi >= coli, 1.0, 0.0).astype(BF16)
    hi, mid, lo = _split3(la_ref[...])
    b = _dot(tri, hi) + _dot(tri, mid) + _dot(tri, lo)
    b_sc[...] = b
    b_last = b_sc[chunk - 1:chunk, :]

    ones = jnp.ones((GLA_DK, chunk), BF16)
    diff = rowi - coli
    a = jnp.zeros((chunk, chunk), F32)
    for d in range(sub):
        k_s = k if d == 0 else pltpu.roll(k, d, axis=0)
        b_s = b if d == 0 else pltpu.roll(b, d, axis=0)
        valid = (row & (sub - 1)) >= d
        prod = jnp.where(valid, q * k_s * jnp.exp(b - b_s), 0.0)
        band = _dot(prod.astype(BF16), ones)
        a = a + jnp.where(diff == d, band, 0.0)

    m = sub
    while m < chunk:
        groups = chunk // (2 * m)
        ref_rows = [jnp.broadcast_to(b_sc[g * 2 * m + m - 1:g * 2 * m + m, :], (2 * m, GLA_DK))
                    for g in range(groups)]
        ref_b = ref_rows[0] if groups == 1 else jnp.concatenate(ref_rows, axis=0)
        e = jnp.exp(-jnp.abs(b - ref_b))
        odd = (row & m) != 0
        qm = jnp.where(odd, q * e, 0.0).astype(BF16)
        km = jnp.where(odd, 0.0, k * e).astype(BF16)
        blk = _dot_nt(qm, km)
        shift = (2 * m).bit_length() - 1
        same = (rowi >> shift) == (coli >> shift)
        a = a + jnp.where(same, blk, 0.0)
        m *= 2

    s_old = s_sc[...]
    o = _dot((q * jnp.exp(b)).astype(BF16), s_old.astype(BF16)) + _dot(a.astype(BF16), v)

    kl = (k * jnp.exp(b_last - b)).astype(BF16)
    decay = jnp.broadcast_to(jnp.exp(b_last), (GLA_DK, GLA_DK)).T
    decay = jnp.concatenate([decay, decay], axis=1)
    s_new = decay * s_old + _dot_tn(kl, v)
    s_sc[...] = s_new

    @pl.when(ci == n_chunks - 1)
    def _():
        sfin_ref[0, 0] = s_new

    r = r_ref[...].astype(F32)
    o_ref[...] = (_rms(o, gn_ref[...]) * _silu(r)).astype(o_ref.dtype)


def gla(p1, la, gla_norm, s0, *, batch, seq):
    m = batch * seq
    chunk = GLA_CHUNK
    n_chunks = seq // chunk
    rows = lambda b, h, c: b * n_chunks + c
    return pl.pallas_call(
        functools.partial(_gla_kernel, chunk=chunk, sub=GLA_SUB),
        grid=(batch, GLA_HEADS, n_chunks),
        in_specs=[
            pl.BlockSpec((chunk, GLA_DK), lambda b, h, c: (rows(b, h, c), P1_GQ // GLA_DK + h)),
            pl.BlockSpec((chunk, GLA_DK), lambda b, h, c: (rows(b, h, c), P1_GK // GLA_DK + h)),
            pl.BlockSpec((chunk, GLA_DV), lambda b, h, c: (rows(b, h, c), P1_GV // GLA_DV + h)),
            pl.BlockSpec((chunk, GLA_DK), lambda b, h, c: (rows(b, h, c), h)),
            pl.BlockSpec((chunk, GLA_DV), lambda b, h, c: (rows(b, h, c), P1_GR // GLA_DV + h)),
            pl.BlockSpec((1, GLA_DV), lambda b, h, c: (0, h)),
            pl.BlockSpec((1, 1, GLA_DK, GLA_DV), lambda b, h, c: (b, h, 0, 0)),
        ],
        out_specs=[
            pl.BlockSpec((chunk, GLA_DV), lambda b, h, c: (rows(b, h, c), h)),
            pl.BlockSpec((1, 1, GLA_DK, GLA_DV), lambda b, h, c: (b, h, 0, 0)),
        ],
        out_shape=[
            jax.ShapeDtypeStruct((m, GLA_WIDTH), BF16),
            jax.ShapeDtypeStruct((batch, GLA_HEADS, GLA_DK, GLA_DV), F32),
        ],
        scratch_shapes=[pltpu.VMEM((GLA_DK, GLA_DV), F32), pltpu.VMEM((chunk, GLA_DK), F32)],
        compiler_params=_params(("parallel", "parallel", "arbitrary")),
        name="gla",
    )(p1, p1, p1, la, p1, gla_norm, s0)


def _ffn_up_prompt_kernel(x_ref, halo_ref, g_ref, wg_ref, wu_ref, cw_ref, cb_ref,
                          h_ref, gtail_ref, xn_sc, *, tm, tiles_per_seq):
    i = pl.program_id(0)
    j = pl.program_id(1)

    @pl.when(j == 0)
    def _():
        xn_sc[0:tm, :] = _rms(x_ref[...], g_ref[...]).astype(BF16)
        keep = jnp.where(i % tiles_per_seq == 0, 0.0, 1.0)
        xn_sc[tm:tm + BF16_ROWS, :] = (_rms(halo_ref[...], g_ref[...]) * keep).astype(BF16)

    g_ext = _dot(xn_sc[...], wg_ref[...])
    u = _dot(xn_sc[0:tm, :], wu_ref[...])
    g = g_ext[0:tm]
    g1 = pltpu.roll(g_ext, 1, axis=0)[0:tm]
    g2 = pltpu.roll(g_ext, 2, axis=0)[0:tm]
    conv = cb_ref[...] + cw_ref[0:1, :] * g2 + cw_ref[1:2, :] * g1 + cw_ref[2:3, :] * g
    h_ref[...] = (_silu(conv) * u).astype(h_ref.dtype)
    gtail_ref[...] = g[tm - SUBLANES:tm]


def ffn_up_prompt(x, gamma, wg, wu, conv_w, conv_b, *, seq, tm, tn):
    m = x.shape[0]
    tiles_per_seq = seq // tm
    halo_blocks = tm // BF16_ROWS
    return pl.pallas_call(
        functools.partial(_ffn_up_prompt_kernel, tm=tm, tiles_per_seq=tiles_per_seq),
        grid=(m // tm, D_FF // tn),
        in_specs=[
            pl.BlockSpec((tm, D_MODEL), lambda i, j: (i, 0)),
            pl.BlockSpec((BF16_ROWS, D_MODEL), lambda i, j: (jnp.maximum(i * halo_blocks - 1, 0), 0)),
            pl.BlockSpec((1, D_MODEL), lambda i, j: (0, 0)),
            pl.BlockSpec((D_MODEL, tn), lambda i, j: (0, j)),
            pl.BlockSpec((D_MODEL, tn), lambda i, j: (0, j)),
            pl.BlockSpec((CONV_W, tn), lambda i, j: (0, j)),
            pl.BlockSpec((1, tn), lambda i, j: (0, j)),
        ],
        out_specs=[
            pl.BlockSpec((tm, tn), lambda i, j: (i, j)),
            pl.BlockSpec((SUBLANES, tn), lambda i, j: (i, j)),
        ],
        out_shape=[
            jax.ShapeDtypeStruct((m, D_FF), BF16),
            jax.ShapeDtypeStruct((m // tm * SUBLANES, D_FF), F32),
        ],
        scratch_shapes=[pltpu.VMEM((tm + BF16_ROWS, D_MODEL), BF16)],
        compiler_params=_params(("parallel", "arbitrary")),
        name="ffn_up_prompt",
    )(x, x, gamma, wg, wu, conv_w, conv_b)


def _ffn_up_sample_kernel(x_ref, g_ref, wg_ref, wu_ref, cw_ref, cb_ref, e1_ref, e2_ref,
                          h_ref, gout_ref, xn_sc, *, seq):
    j = pl.program_id(0)

    @pl.when(j == 0)
    def _():
        xn_sc[...] = _rms(x_ref[...], g_ref[...]).astype(BF16)

    g = _dot(xn_sc[...], wg_ref[...])
    u = _dot(xn_sc[...], wu_ref[...])
    pos = lax.broadcasted_iota(jnp.int32, (g.shape[0], 1), 0) & (seq - 1)
    g1 = jnp.where(pos >= 1, pltpu.roll(g, 1, axis=0), e1_ref[...])
    g2 = jnp.where(pos >= 2, pltpu.roll(g, 2, axis=0), e2_ref[...])
    conv = cb_ref[...] + cw_ref[0:1, :] * g2 + cw_ref[1:2, :] * g1 + cw_ref[2:3, :] * g
    h_ref[...] = (_silu(conv) * u).astype(h_ref.dtype)
    gout_ref[...] = g


def ffn_up_sample(x, gamma, wg, wu, conv_w, conv_b, e1, e2, *, seq, tn):
    m = x.shape[0]
    return pl.pallas_call(
        functools.partial(_ffn_up_sample_kernel, seq=seq),
        grid=(D_FF // tn,),
        in_specs=[
            pl.BlockSpec((m, D_MODEL), lambda j: (0, 0)),
            pl.BlockSpec((1, D_MODEL), lambda j: (0, 0)),
            pl.BlockSpec((D_MODEL, tn), lambda j: (0, j)),
            pl.BlockSpec((D_MODEL, tn), lambda j: (0, j)),
            pl.BlockSpec((CONV_W, tn), lambda j: (0, j)),
            pl.BlockSpec((1, tn), lambda j: (0, j)),
            pl.BlockSpec((m, tn), lambda j: (0, j)),
            pl.BlockSpec((m, tn), lambda j: (0, j)),
        ],
        out_specs=[
            pl.BlockSpec((m, tn), lambda j: (0, j)),
            pl.BlockSpec((m, tn), lambda j: (0, j)),
        ],
        out_shape=[
            jax.ShapeDtypeStruct((m, D_FF), BF16),
            jax.ShapeDtypeStruct((m, D_FF), F32),
        ],
        scratch_shapes=[pltpu.VMEM((m, D_MODEL), BF16)],
        compiler_params=_params(("arbitrary",)),
        name="ffn_up_sample",
    )(x, gamma, wg, wu, conv_w, conv_b, e1, e2)


def _rmsnorm_kernel(x_ref, g_ref, o_ref):
    o_ref[...] = _rms(x_ref[...], g_ref[...])


def rmsnorm(x, gamma, *, tm):
    m = x.shape[0]
    return pl.pallas_call(
        _rmsnorm_kernel,
        grid=(m // tm,),
        in_specs=[pl.BlockSpec((tm, D_MODEL), lambda i: (i, 0)),
                  pl.BlockSpec((1, D_MODEL), lambda i: (0, 0))],
        out_specs=pl.BlockSpec((tm, D_MODEL), lambda i: (i, 0)),
        out_shape=jax.ShapeDtypeStruct((m, D_MODEL), F32),
        compiler_params=_params(("parallel",)),
        name="rmsnorm",
    )(x, gamma)


def _prep_layer(w_in_l, b_f_l, w_a2_l, b_a_l, w_out_l, w_gate_l, w_up_l, w_down_l):
    o_fq, o_fk, o_fv = 0, FOX_WIDTH, 2 * FOX_WIDTH
    o_fl = 3 * FOX_WIDTH
    o_gq = o_fl + FOX_HEADS
    o_gk = o_gq + GLA_KEY_WIDTH
    o_gv = o_gk + GLA_KEY_WIDTH
    o_gr = o_gv + GLA_WIDTH
    o_ga = o_gr + GLA_WIDTH
    cols = lambda a, n: w_in_l[:, a:a + n]
    w_p1 = jnp.concatenate([cols(o_fq, FOX_WIDTH), cols(o_gq, GLA_KEY_WIDTH), cols(o_gk, GLA_KEY_WIDTH),
                            cols(o_gv, GLA_WIDTH), cols(o_gr, GLA_WIDTH)], axis=1).astype(BF16)
    w_kv = cols(o_fk, 2 * FOX_WIDTH).astype(BF16)
    w_fl = cols(o_fl, FOX_HEADS)
    w_ga = cols(o_ga, GLA_RANK)
    pad = LANES - FOX_HEADS - GLA_RANK
    w_small = jnp.concatenate([w_fl, w_ga, jnp.zeros((D_MODEL, pad), F32)], axis=1).astype(BF16)
    b_small = jnp.concatenate([b_f_l, jnp.zeros((LANES - FOX_HEADS,), F32)]).reshape(1, LANES)
    wa2_pad = jnp.concatenate([jnp.zeros((FOX_HEADS, GLA_KEY_WIDTH), F32), w_a2_l,
                               jnp.zeros((pad, GLA_KEY_WIDTH), F32)], axis=0).astype(BF16)
    return dict(
        w_p1=w_p1, w_kv=w_kv, w_small=w_small, b_small=b_small, wa2_pad=wa2_pad,
        b_a=b_a_l.reshape(1, GLA_KEY_WIDTH),
        wf_t=jnp.pad(w_fl.T, ((0, BF16_ROWS - FOX_HEADS), (0, 0))).astype(BF16),
        bf_t=jnp.pad(b_f_l.reshape(FOX_HEADS, 1), ((0, BF16_ROWS - FOX_HEADS), (0, 0))),
        w_out_f=w_out_l[:FOX_WIDTH].astype(BF16), w_out_g=w_out_l[FOX_WIDTH:].astype(BF16),
        w_gate=w_gate_l.astype(BF16), w_up=w_up_l.astype(BF16), w_down=w_down_l.astype(BF16),
    )


def _pad_rows(x, batch, t, t_pad):
    w = x.shape[-1]
    return jnp.pad(x.reshape(batch, t, w), ((0, 0), (0, t_pad - t), (0, 0))).reshape(batch * t_pad, w)


def kernel(x_prompt, x_sample, cache_k, cache_v, cache_logf, state_gla, state_conv, page_table,
           w_in, b_f, w_a2, b_a, fox_norm, gla_norm, w_out, norm_attn, norm_ffn,
           w_gate, w_up, conv_w, conv_b, w_down, norm_final):
    bp, seq, _ = x_prompt.shape
    db, t_new, _ = x_sample.shape
    n_pool = cache_k.shape[1]
    mp, ms = bp * seq, db * t_new

    xp = x_prompt.reshape(mp, D_MODEL)
    xs = x_sample.reshape(ms, D_MODEL)
    cache_k4 = cache_k.reshape(DEPTH, n_pool, PAGE * FOX_HEADS, FOX_DIM)
    cache_v4 = cache_v.reshape(DEPTH, n_pool, PAGE * FOX_HEADS, FOX_DIM)
    cache_lft = jnp.swapaxes(cache_logf, 2, 3)
    scale = FOX_DIM ** -0.5
    col_scale = jnp.concatenate([jnp.full((FOX_WIDTH,), scale, F32),
                                 jnp.full((GLA_KEY_WIDTH,), GLA_DK ** -0.5, F32),
                                 jnp.ones((P1_WIDTH - FOX_WIDTH - GLA_KEY_WIDTH,), F32)]).reshape(1, P1_WIDTH)
    zero_state = jnp.zeros((bp, GLA_HEADS, GLA_DK, GLA_DV), F32)

    outs = {k: [] for k in ("kp", "vp", "lp", "sp", "cp", "ks", "vs", "ls", "ss", "cs")}
    for l in range(DEPTH):
        w = _prep_layer(w_in[l], b_f[l], w_a2[l], b_a[l], w_out[l], w_gate[l], w_up[l], w_down[l])
        g_attn = norm_attn[l].reshape(1, D_MODEL)
        g_ffn = norm_ffn[l].reshape(1, D_MODEL)
        fn = fox_norm[l].reshape(1, FOX_WIDTH)
        gn = gla_norm[l].reshape(1, GLA_WIDTH)
        cb = conv_b[l].reshape(1, D_FF)

        xn, logf, la, c = norm_small(xp, g_attn, w["w_small"], w["b_small"], w["wa2_pad"], w["b_a"],
                                     w["wf_t"], w["bf_t"], tm=512, seg=seq)
        (p1,) = matmul([xn], [w["w_p1"]], out_dtypes=[BF16], tm=1024, tn=512, scale=col_scale)
        kv32, kv16 = matmul([xn], [w["w_kv"]], out_dtypes=[F32, BF16], tm=1024, tn=512)
        fo = fox_prompt(p1, kv16, c, fn, batch=bp, seq=seq, tq=512)
        go, s_fin = gla(p1, la, gn, zero_state, batch=bp, seq=seq)
        (xp,) = matmul([fo, go], [w["w_out_f"], w["w_out_g"]], out_dtypes=[F32], tm=1024, tn=512, res=xp)
        h, gtail = ffn_up_prompt(xp, g_ffn, w["w_gate"], w["w_up"], conv_w[l], cb, seq=seq, tm=512, tn=512)
        (xp,) = matmul([h], [w["w_down"]], out_dtypes=[F32], tm=512, tn=512, res=xp)
        outs["kp"].append(kv32[:, :FOX_WIDTH].reshape(bp, seq, FOX_HEADS, FOX_DIM))
        outs["vp"].append(kv32[:, FOX_WIDTH:].reshape(bp, seq, FOX_HEADS, FOX_DIM))
        outs["lp"].append(logf.reshape(bp, seq, FOX_HEADS))
        outs["sp"].append(s_fin)
        tiles_per_seq = seq // 512
        outs["cp"].append(gtail.reshape(bp, tiles_per_seq, SUBLANES, D_FF)[:, -1, SUBLANES - (CONV_W - 1):, :])

        xn, logf, la, c = norm_small(xs, g_attn, w["w_small"], w["b_small"], w["wa2_pad"], w["b_a"],
                                     w["wf_t"], w["bf_t"], tm=ms, seg=t_new)
        (p1,) = matmul([xn], [w["w_p1"]], out_dtypes=[BF16], tm=ms, tn=512, scale=col_scale)
        kv32, kv16 = matmul([xn], [w["w_kv"]], out_dtypes=[F32, BF16], tm=ms, tn=512)
        kn_pad = _pad_rows(kv16[:, :FOX_WIDTH], db, t_new, PAGE)
        vn_pad = _pad_rows(kv16[:, FOX_WIDTH:], db, t_new, PAGE)
        c_new = c.reshape(FOX_HEADS, db, t_new).transpose(1, 0, 2)
        c_new_pad = jnp.pad(c_new, ((0, 0), (0, 0), (0, PAGE - t_new)))
        fo = fox_sample(page_table, p1[:, :FOX_WIDTH].astype(F32), kn_pad, vn_pad, c_new_pad, fn,
                        cache_k4, cache_v4, cache_lft, layer=l, pages=8)
        p1_pad = _pad_rows(p1, db, t_new, GLA_CHUNK)
        la_pad = _pad_rows(la, db, t_new, GLA_CHUNK)
        go_pad, s_fin = gla(p1_pad, la_pad, gn, state_gla[l], batch=db, seq=GLA_CHUNK)
        go = go_pad.reshape(db, GLA_CHUNK, GLA_WIDTH)[:, :t_new].reshape(ms, GLA_WIDTH)
        (xs,) = matmul([fo.astype(BF16), go], [w["w_out_f"], w["w_out_g"]], out_dtypes=[F32],
                       tm=ms, tn=512, res=xs)
        st = state_conv[l]
        zeros_f = jnp.zeros((db, t_new - 1, D_FF), F32)
        e1 = jnp.concatenate([st[:, 1:2], zeros_f], axis=1).reshape(ms, D_FF)
        e2 = jnp.concatenate([st[:, 0:2], zeros_f[:, 1:]], axis=1).reshape(ms, D_FF)
        h, g_all = ffn_up_sample(xs, g_ffn, w["w_gate"], w["w_up"], conv_w[l], cb, e1, e2, seq=t_new, tn=512)
        (xs,) = matmul([h], [w["w_down"]], out_dtypes=[F32], tm=ms, tn=512, res=xs)
        outs["ks"].append(kv32[:, :FOX_WIDTH].reshape(db, t_new, FOX_HEADS, FOX_DIM))
        outs["vs"].append(kv32[:, FOX_WIDTH:].reshape(db, t_new, FOX_HEADS, FOX_DIM))
        outs["ls"].append(logf.reshape(db, t_new, FOX_HEADS))
        outs["ss"].append(s_fin)
        outs["cs"].append(g_all.reshape(db, t_new, D_FF)[:, t_new - (CONV_W - 1):, :])

    g_fin = norm_final.reshape(1, D_MODEL)
    y_prompt = rmsnorm(xp, g_fin, tm=512).reshape(bp, seq, D_MODEL)
    y_sample = rmsnorm(xs, g_fin, tm=ms).reshape(db, t_new, D_MODEL)
    stack = lambda key: jnp.stack(outs[key])
    return (y_prompt, y_sample,
            stack("kp"), stack("vp"), stack("lp"), stack("sp"), stack("cp"),
            stack("ks"), stack("vs"), stack("ls"), stack("ss"), stack("cs"))
```

```python
import functools
import math

import jax
import jax.numpy as jnp
from jax import lax
from jax.experimental import pallas as pl
from jax.experimental.pallas import tpu as pltpu

F32 = jnp.float32
BF16 = jnp.bfloat16

D_MODEL = 2048
DEPTH = 4
PAGE = 128
FOX_HEADS = 8
FOX_DIM = 128
FOX_WIDTH = FOX_HEADS * FOX_DIM
GLA_HEADS = 4
GLA_DK = 128
GLA_DV = 256
GLA_KEY_WIDTH = GLA_HEADS * GLA_DK
GLA_WIDTH = GLA_HEADS * GLA_DV
GLA_RANK = 16
GLA_NORMALIZER = 16.0
D_FF = 5632
CONV_W = 3
EPS = 1e-6
LOG2E = math.log2(math.e)

LANES = 128
SUBLANES = 8
BF16_ROWS = 16
MXU_COLS = 256
VMEM_LIMIT = 56 * 1024 * 1024

GLA_CHUNK = 128
GLA_SUB = 8
GLA_HEAD_BLOCK = 2
NEG_BIG = -1e30

TM = 1024
TN = 512
TQ = 512
SAMPLE_PAGES = 8

G_GQ, G_GK, G_GV, G_GR = 0, 512, 1024, 2048
G_WIDTH = 3072


def _params(sem):
    return pltpu.CompilerParams(dimension_semantics=sem, vmem_limit_bytes=VMEM_LIMIT)


def _log_sigmoid(x):
    return jnp.minimum(x, 0.0) - jnp.log(1.0 + jnp.exp(-jnp.abs(x)))


def _silu(x):
    return x * (1.0 / (1.0 + jnp.exp(-x)))


def _rms(x, g):
    return x * lax.rsqrt(jnp.mean(x * x, axis=-1, keepdims=True) + EPS) * g


def _dot(a, b):
    return jnp.dot(a, b, preferred_element_type=F32)


def _dot_nt(a, b):
    return lax.dot_general(a, b, (((1,), (1,)), ((), ())), preferred_element_type=F32)


def _dot_tn(a, b):
    return lax.dot_general(a, b, (((0,), (0,)), ((), ())), preferred_element_type=F32)


def _split3(x):
    hi = x.astype(BF16).astype(F32)
    r1 = x - hi
    mid = r1.astype(BF16).astype(F32)
    lo = (r1 - mid).astype(BF16).astype(F32)
    return hi, mid, lo


def _norm_small_kernel(x_ref, g_ref, ws_ref, bs_ref, wa2_ref, ba_ref, wft_ref, bft_ref,
                       xn_ref, logf_ref, la_ref, c_ref, *rest, tm, seg, emit_cx):
    if emit_cx:
        cx_ref, carry_ref = rest
    else:
        (carry_ref,) = rest
    i = pl.program_id(0)
    xn = _rms(x_ref[...], g_ref[0]).astype(BF16)
    xn_ref[...] = xn
    zs = _dot(xn, ws_ref[0])
    logf_ref[...] = _log_sigmoid(zs + bs_ref[0])[:, :FOX_HEADS]
    la_pre = _dot(zs.astype(BF16), wa2_ref[0]) + ba_ref[0]
    la_ref[...] = _log_sigmoid(la_pre) * (1.0 / GLA_NORMALIZER)

    lft = _log_sigmoid(_dot_nt(wft_ref[0], xn) + bft_ref[0])[:FOX_HEADS]
    lane = lax.broadcasted_iota(jnp.int32, (FOX_HEADS, tm), 1)
    pos = lane & (seg - 1) if seg < tm else lane
    c = lft
    d = 1
    while d < min(seg, tm):
        c = c + jnp.where(pos >= d, pltpu.roll(c, d, axis=1), 0.0)
        d *= 2
    if seg > tm:
        tiles_per_seq = seg // tm

        @pl.when(i % tiles_per_seq == 0)
        def _():
            carry_ref[...] = jnp.zeros_like(carry_ref)

        c = c + carry_ref[:, 0:1]
        carry_ref[...] = jnp.broadcast_to(c[:, tm - 1:tm], carry_ref.shape)
    c_ref[0] = c

    if emit_cx:
        c2 = c * (-LOG2E)
        stacked = jnp.concatenate(
            [c2, c2, c2, jnp.zeros((LANES - 3 * FOX_HEADS, tm), F32)], axis=0)
        hi, mid, lo = _split3(stacked)
        rowid = lax.broadcasted_iota(jnp.int32, stacked.shape, 0)
        pieces = jnp.where(rowid < FOX_HEADS, hi, jnp.where(rowid < 2 * FOX_HEADS, mid, lo))
        cx_ref[...] = pieces.T.astype(BF16)


def norm_small(x, gamma, w_small, b_small, wa2_pad, b_a, wf_t, bf_t, *, layer, tm, seg, emit_cx):
    m = x.shape[0]
    n_tiles = m // tm
    if seg >= tm:
        tps = seg // tm
        c_shape = (m // seg, FOX_HEADS, seg)
        c_map = lambda i: (i // tps, 0, i % tps)
    else:
        c_shape = (n_tiles, FOX_HEADS, tm)
        c_map = lambda i: (i, 0, 0)
    lay = lambda i: (layer, 0, 0)
    out_specs = [
        pl.BlockSpec((tm, D_MODEL), lambda i: (i, 0)),
        pl.BlockSpec((tm, FOX_HEADS), lambda i: (i, 0)),
        pl.BlockSpec((tm, GLA_KEY_WIDTH), lambda i: (i, 0)),
        pl.BlockSpec((1, FOX_HEADS, tm), c_map),
    ]
    out_shape = [
        jax.ShapeDtypeStruct((m, D_MODEL), BF16),
        jax.ShapeDtypeStruct((m, FOX_HEADS), F32),
        jax.ShapeDtypeStruct((m, GLA_KEY_WIDTH), F32),
        jax.ShapeDtypeStruct(c_shape, F32),
    ]
    if emit_cx:
        out_specs.append(pl.BlockSpec((tm, LANES), lambda i: (i, 0)))
        out_shape.append(jax.ShapeDtypeStruct((m, LANES), BF16))
    return pl.pallas_call(
        functools.partial(_norm_small_kernel, tm=tm, seg=seg, emit_cx=emit_cx),
        grid=(n_tiles,),
        in_specs=[
            pl.BlockSpec((tm, D_MODEL), lambda i: (i, 0)),
            pl.BlockSpec((1, 1, D_MODEL), lay),
            pl.BlockSpec((1, D_MODEL, LANES), lay),
            pl.BlockSpec((1, 1, LANES), lay),
            pl.BlockSpec((1, LANES, GLA_KEY_WIDTH), lay),
            pl.BlockSpec((1, 1, GLA_KEY_WIDTH), lay),
            pl.BlockSpec((1, BF16_ROWS, D_MODEL), lay),
            pl.BlockSpec((1, BF16_ROWS, 1), lay),
        ],
        out_specs=out_specs,
        out_shape=out_shape,
        scratch_shapes=[pltpu.VMEM((FOX_HEADS, LANES), F32)],
        compiler_params=_params(("arbitrary",)),
        name="norm_small",
    )(x, gamma, w_small, b_small, wa2_pad, b_a, wf_t, bf_t)


def _mm_kernel(*refs, n_a, has_scale, has_res, n_out):
    a_refs = refs[:n_a]
    w_refs = refs[n_a:2 * n_a]
    pos = 2 * n_a
    acc = _dot(a_refs[0][...], w_refs[0][0])
    for a_ref, w_ref in zip(a_refs[1:], w_refs[1:]):
        acc = acc + _dot(a_ref[...], w_ref[0])
    if has_scale:
        acc = acc * refs[pos][...]
        pos += 1
    if has_res:
        acc = refs[pos][...] + acc
        pos += 1
    for o_ref in refs[pos:pos + n_out]:
        o_ref[...] = acc.astype(o_ref.dtype)


def matmul(a_list, w_list, *, layer, out_dtypes, tm, tn, n, w_rows=None, scale=None, res=None):
    m = a_list[0].shape[0]
    w_rows = w_rows or [0] * len(a_list)
    in_specs = [pl.BlockSpec((tm, a.shape[1]), lambda i, j: (i, 0)) for a in a_list]
    for a, w, r in zip(a_list, w_list, w_rows):
        in_specs.append(pl.BlockSpec((1, a.shape[1], tn), lambda i, j, r=r: (layer, r, j)))
    args = list(a_list) + list(w_list)
    if scale is not None:
        in_specs.append(pl.BlockSpec((1, tn), lambda i, j: (0, j)))
        args.append(scale)
    if res is not None:
        in_specs.append(pl.BlockSpec((tm, tn), lambda i, j: (i, j)))
        args.append(res)
    outs = pl.pallas_call(
        functools.partial(_mm_kernel, n_a=len(a_list), has_scale=scale is not None,
                          has_res=res is not None, n_out=len(out_dtypes)),
        grid=(m // tm, n // tn),
        in_specs=in_specs,
        out_specs=[pl.BlockSpec((tm, tn), lambda i, j: (i, j)) for _ in out_dtypes],
        out_shape=[jax.ShapeDtypeStruct((m, n), dt) for dt in out_dtypes],
        compiler_params=_params(("parallel", "arbitrary")),
        name="matmul",
    )(*args)
    return outs


def _fox_prompt_kernel(q_ref, k_ref, cx_ref, v_ref, gn_ref, o_ref, kx_sc, vt_sc, *, seq, tq):
    h = pl.program_id(1)
    kx_sc[:, 0:FOX_DIM] = k_ref[...]
    kx_sc[:, FOX_DIM:2 * FOX_DIM] = cx_ref[...]
    vt_sc[0:FOX_DIM, :] = v_ref[...].T.astype(BF16)
    vt_sc[FOX_DIM:FOX_DIM + BF16_ROWS, :] = jnp.ones((BF16_ROWS, seq), BF16)

    lane = lax.broadcasted_iota(jnp.int32, (tq, FOX_DIM), 1)
    pick = (lane == h) | (lane == h + FOX_HEADS) | (lane == h + 2 * FOX_HEADS)
    onehot = jnp.where(pick, 1.0, 0.0).astype(BF16)
    key_i = lax.broadcasted_iota(jnp.int32, (tq, tq), 0)
    qry_i = lax.broadcasted_iota(jnp.int32, (tq, tq), 1)
    causal = key_i <= qry_i

    for qi in range(seq // tq):
        lo, hi = qi * tq, (qi + 1) * tq
        qx = jnp.concatenate([q_ref[lo:hi, :], onehot], axis=1)
        s_diag = jnp.where(causal, _dot_nt(kx_sc[lo:hi, :], qx), NEG_BIG)
        m = jnp.max(s_diag, axis=0, keepdims=True)
        if qi > 0:
            s_off = _dot_nt(kx_sc[0:lo, :], qx)
            m = jnp.maximum(m, jnp.max(s_off, axis=0, keepdims=True))
            p_off = jnp.exp2(s_off - m).astype(BF16)
            ot = _dot(vt_sc[:, 0:lo], p_off)
        p_diag = jnp.exp2(s_diag - m).astype(BF16)
        ot_d = _dot(vt_sc[:, lo:hi], p_diag)
        ot = ot + ot_d if qi > 0 else ot_d
        o = ot[0:FOX_DIM, :] * (1.0 / ot[FOX_DIM:FOX_DIM + 1, :])
        y = o * lax.rsqrt(jnp.mean(o * o, axis=0, keepdims=True) + EPS)
        o_ref[lo:hi, :] = (y.T * gn_ref[0]).astype(o_ref.dtype)


def fox_prompt(q, k16, cx, v32, fox_norm, *, layer, batch, seq):
    m = batch * seq
    return pl.pallas_call(
        functools.partial(_fox_prompt_kernel, seq=seq, tq=TQ),
        grid=(batch, FOX_HEADS),
        in_specs=[
            pl.BlockSpec((seq, FOX_DIM), lambda b, h: (b, h)),
            pl.BlockSpec((seq, FOX_DIM), lambda b, h: (b, h)),
            pl.BlockSpec((seq, LANES), lambda b, h: (b, 0)),
            pl.BlockSpec((seq, FOX_DIM), lambda b, h: (b, h)),
            pl.BlockSpec((1, 1, FOX_DIM), lambda b, h: (layer, 0, h)),
        ],
        out_specs=pl.BlockSpec((seq, FOX_DIM), lambda b, h: (b, h)),
        out_shape=jax.ShapeDtypeStruct((m, FOX_WIDTH), BF16),
        scratch_shapes=[pltpu.VMEM((seq, 2 * FOX_DIM), BF16),
                        pltpu.VMEM((FOX_DIM + BF16_ROWS, seq), BF16)],
        compiler_params=_params(("parallel", "arbitrary")),
        name="fox_prompt",
    )(q, k16, cx, v32, fox_norm)


def _fox_sample_kernel(pt_ref, q_ref, kn_ref, vn_ref, cn_ref, gn_ref, *rest, pages, t_new):
    k_refs = rest[:pages]
    v_refs = rest[pages:2 * pages]
    lf_refs = rest[2 * pages:3 * pages]
    o_ref, m_sc, l_sc, acc_sc, carry_sc = rest[3 * pages:]
    del pt_ref
    step = pl.program_id(1)
    n_steps = pl.num_programs(1)
    rows = t_new

    @pl.when(step == 0)
    def _():
        m_sc[...] = jnp.full_like(m_sc, NEG_BIG)
        l_sc[...] = jnp.zeros_like(l_sc)
        acc_sc[...] = jnp.zeros_like(acc_sc)
        carry_sc[...] = jnp.zeros_like(carry_sc)

    lane = lax.broadcasted_iota(jnp.int32, (FOX_HEADS, PAGE), 1)
    bias = [None] * pages
    carry = carry_sc[:, 0:1]
    for p in range(pages - 1, -1, -1):
        lf = lf_refs[p][0, 0]
        suf = lf
        d = 1
        while d < PAGE:
            suf = suf + jnp.where(lane < PAGE - d, pltpu.roll(suf, PAGE - d, axis=1), 0.0)
            d *= 2
        bias[p] = ((suf - lf) + carry) * LOG2E
        carry = carry + suf[:, 0:1]
    carry_sc[...] = jnp.broadcast_to(carry, carry_sc.shape)

    def attend(score_fn, value_fn, n_blocks):
        s = jnp.concatenate(
            [jnp.concatenate([score_fn(h, p) for p in range(n_blocks)], axis=1)
             for h in range(FOX_HEADS)], axis=0)
        m_old = m_sc[...]
        m_new = jnp.maximum(m_old, jnp.max(s, axis=-1, keepdims=True))
        alpha = jnp.exp2(m_old - m_new)
        pr = jnp.exp2(s - m_new)
        l_sc[...] = alpha * l_sc[...] + jnp.sum(pr, axis=-1, keepdims=True)
        m_sc[...] = m_new
        for h in range(FOX_HEADS):
            sl = slice(h * rows, (h + 1) * rows)
            acc = alpha[sl, :] * acc_sc[sl, :]
            for p in range(n_blocks):
                pr_hp = pr[sl, p * PAGE:(p + 1) * PAGE].astype(BF16)
                acc = acc + _dot(pr_hp, value_fn(h, p))
            acc_sc[sl, :] = acc

    def q_head(h):
        return q_ref[:, h * FOX_DIM:(h + 1) * FOX_DIM].astype(BF16)

    def page_scores(h, p):
        k_h = k_refs[p][0, 0, pl.ds(h, PAGE, stride=FOX_HEADS), :].astype(BF16)
        return _dot_nt(q_head(h), k_h) + bias[p][h:h + 1, :]

    def page_values(h, p):
        return v_refs[p][0, 0, pl.ds(h, PAGE, stride=FOX_HEADS), :].astype(BF16)

    attend(page_scores, page_values, pages)

    @pl.when(step == n_steps - 1)
    def _():
        row = lax.broadcasted_iota(jnp.int32, (rows, PAGE), 0)
        col = lax.broadcasted_iota(jnp.int32, (rows, PAGE), 1)
        valid = (col <= row) & (col < t_new)

        def new_scores(h, p):
            cols = slice(h * FOX_DIM, (h + 1) * FOX_DIM)
            s = _dot_nt(q_head(h), kn_ref[:, cols]) - cn_ref[0, h:h + 1, :] * LOG2E
            return jnp.where(valid, s, NEG_BIG)

        def new_values(h, p):
            return vn_ref[:, h * FOX_DIM:(h + 1) * FOX_DIM]

        attend(new_scores, new_values, 1)
        for h in range(FOX_HEADS):
            cols = slice(h * FOX_DIM, (h + 1) * FOX_DIM)
            sl = slice(h * rows, (h + 1) * rows)
            o = acc_sc[sl, :] / l_sc[sl, :]
            o_ref[:, cols] = _rms(o, gn_ref[0, :, cols])


def fox_sample(page_table, q, k_new_pad, v_new_pad, c_new_pad, fox_norm, cache_k4, cache_v4,
               cache_lft, *, layer, pages):
    db, n_pages = page_table.shape
    t_new = q.shape[0] // db
    n_steps = n_pages // pages
    pt_flat = page_table.reshape(-1)

    def page_map(p):
        def index(b, s, pt):
            return (layer, pt[b * n_pages + (n_steps - 1 - s) * pages + p], 0, 0)
        return index

    in_specs = [
        pl.BlockSpec((t_new, FOX_WIDTH), lambda b, s, pt: (b, 0)),
        pl.BlockSpec((PAGE, FOX_WIDTH), lambda b, s, pt: (b, 0)),
        pl.BlockSpec((PAGE, FOX_WIDTH), lambda b, s, pt: (b, 0)),
        pl.BlockSpec((1, FOX_HEADS, PAGE), lambda b, s, pt: (b, 0, 0)),
        pl.BlockSpec((1, 1, FOX_WIDTH), lambda b, s, pt: (layer, 0, 0)),
    ]
    in_specs += [pl.BlockSpec((1, 1, PAGE * FOX_HEADS, FOX_DIM), page_map(p)) for p in range(pages)]
    in_specs += [pl.BlockSpec((1, 1, PAGE * FOX_HEADS, FOX_DIM), page_map(p)) for p in range(pages)]
    in_specs += [pl.BlockSpec((1, 1, FOX_HEADS, PAGE), page_map(p)) for p in range(pages)]
    grid_spec = pltpu.PrefetchScalarGridSpec(
        num_scalar_prefetch=1,
        grid=(db, n_steps),
        in_specs=in_specs,
        out_specs=pl.BlockSpec((t_new, FOX_WIDTH), lambda b, s, pt: (b, 0)),
        scratch_shapes=[pltpu.VMEM((FOX_HEADS * t_new, 1), F32),
                        pltpu.VMEM((FOX_HEADS * t_new, 1), F32),
                        pltpu.VMEM((FOX_HEADS * t_new, FOX_DIM), F32),
                        pltpu.VMEM((FOX_HEADS, LANES), F32)],
    )
    return pl.pallas_call(
        functools.partial(_fox_sample_kernel, pages=pages, t_new=t_new),
        grid_spec=grid_spec,
        out_shape=jax.ShapeDtypeStruct((db * t_new, FOX_WIDTH), F32),
        compiler_params=_params(("parallel", "arbitrary")),
        name="fox_sample",
    )(pt_flat, q, k_new_pad, v_new_pad, c_new_pad, fox_norm,
      *([cache_k4] * pages), *([cache_v4] * pages), *([cache_lft] * pages))


def _gla_kernel(q_ref, k_ref, v_ref, la_ref, r_ref, gn_ref, s0_ref, o_ref, sfin_ref,
                s_sc, b_sc, *, chunk, sub, hb):
    ci = pl.program_id(2)
    n_chunks = pl.num_programs(2)
    heads = range(hb)
    kcols = lambda h: slice(h * GLA_DK, (h + 1) * GLA_DK)
    vcols = lambda h: slice(h * GLA_DV, (h + 1) * GLA_DV)

    @pl.when(ci == 0)
    def _():
        s_sc[...] = s0_ref[0]

    q = [q_ref[:, kcols(h)].astype(F32) for h in heads]
    k = [k_ref[:, kcols(h)].astype(F32) for h in heads]
    v = [v_ref[:, vcols(h)] for h in heads]
    rowi = lax.broadcasted_iota(jnp.int32, (chunk, chunk), 0)
    coli = lax.broadcasted_iota(jnp.int32, (chunk, chunk), 1)
    row = lax.broadcasted_iota(jnp.int32, (chunk, 1), 0)

    tri = jnp.where(rowi >= coli, 1.0, 0.0).astype(BF16)
    hi, mid, lo = _split3(la_ref[...])
    b_all = _dot(tri, hi.astype(BF16)) + _dot(tri, mid.astype(BF16)) + _dot(tri, lo.astype(BF16))
    b_sc[...] = b_all
    b = [b_all[:, kcols(h)] for h in heads]
    b_last = [b_sc[chunk - 1:chunk, kcols(h)] for h in heads]

    ones = jnp.ones((GLA_DK, chunk), BF16)
    diff = rowi - coli
    sub_shift = sub.bit_length() - 1
    a = [jnp.zeros((chunk, chunk), F32) for _ in heads]
    for d in range(sub):
        for h in heads:
            k_s = k[h] if d == 0 else pltpu.roll(k[h], d, axis=0)
            b_s = b[h] if d == 0 else pltpu.roll(b[h], d, axis=0)
            prod = q[h] * k_s * jnp.exp(b[h] - b_s)
            band = _dot(prod.astype(BF16), ones)
            a[h] = a[h] + jnp.where(diff == d, band, 0.0)
    in_sub = (rowi >> sub_shift) == (coli >> sub_shift)
    a = [jnp.where(in_sub, a[h], 0.0) for h in heads]

    m = sub
    while m < chunk:
        groups = chunk // (2 * m)
        odd = (row & m) != 0
        shift = (2 * m).bit_length() - 1
        same = (rowi >> shift) == (coli >> shift)
        for h in heads:
            ref_rows = [jnp.broadcast_to(b_sc[g * 2 * m + m - 1:g * 2 * m + m, kcols(h)], (2 * m, GLA_DK))
                        for g in range(groups)]
            ref_b = ref_rows[0] if groups == 1 else jnp.concatenate(ref_rows, axis=0)
            e = jnp.exp(-jnp.abs(b[h] - ref_b))
            qm = jnp.where(odd, q[h] * e, 0.0).astype(BF16)
            km = jnp.where(odd, 0.0, k[h] * e).astype(BF16)
            a[h] = a[h] + jnp.where(same, _dot_nt(qm, km), 0.0)
        m *= 2

    s_old = [s_sc[h] for h in heads]
    o = [_dot((q[h] * jnp.exp(b[h])).astype(BF16), s_old[h].astype(BF16))
         + _dot(a[h].astype(BF16), v[h]) for h in heads]

    s_new = []
    for h in heads:
        kl = (k[h] * jnp.exp(b_last[h] - b[h])).astype(BF16)
        decay = jnp.broadcast_to(jnp.exp(b_last[h]), (GLA_DK, GLA_DK)).T
        decay = jnp.concatenate([decay, decay], axis=1)
        s_new.append(decay * s_old[h] + _dot_tn(kl, v[h]))
        s_sc[h] = s_new[h]

    @pl.when(ci == n_chunks - 1)
    def _():
        for h in heads:
            sfin_ref[0, h] = s_new[h]

    for h in heads:
        r = r_ref[:, vcols(h)].astype(F32)
        o_ref[:, vcols(h)] = (_rms(o[h], gn_ref[0, :, vcols(h)]) * _silu(r)).astype(o_ref.dtype)


def gla(gp, la, gla_norm, s0, *, layer, batch, seq):
    m = batch * seq
    chunk = GLA_CHUNK
    hb = GLA_HEAD_BLOCK
    kw, vw = hb * GLA_DK, hb * GLA_DV
    n_chunks = seq // chunk
    rows = lambda b, h, c: b * n_chunks + c
    return pl.pallas_call(
        functools.partial(_gla_kernel, chunk=chunk, sub=GLA_SUB, hb=hb),
        grid=(batch, GLA_HEADS // hb, n_chunks),
        in_specs=[
            pl.BlockSpec((chunk, kw), lambda b, h, c: (rows(b, h, c), G_GQ // kw + h)),
            pl.BlockSpec((chunk, kw), lambda b, h, c: (rows(b, h, c), G_GK // kw + h)),
            pl.BlockSpec((chunk, vw), lambda b, h, c: (rows(b, h, c), G_GV // vw + h)),
            pl.BlockSpec((chunk, kw), lambda b, h, c: (rows(b, h, c), h)),
            pl.BlockSpec((chunk, vw), lambda b, h, c: (rows(b, h, c), G_GR // vw + h)),
            pl.BlockSpec((1, 1, vw), lambda b, h, c: (layer, 0, h)),
            pl.BlockSpec((1, hb, GLA_DK, GLA_DV), lambda b, h, c: (b, h, 0, 0)),
        ],
        out_specs=[
            pl.BlockSpec((chunk, vw), lambda b, h, c: (rows(b, h, c), h)),
            pl.BlockSpec((1, hb, GLA_DK, GLA_DV), lambda b, h, c: (b, h, 0, 0)),
        ],
        out_shape=[
            jax.ShapeDtypeStruct((m, GLA_WIDTH), BF16),
            jax.ShapeDtypeStruct((batch, GLA_HEADS, GLA_DK, GLA_DV), F32),
        ],
        scratch_shapes=[pltpu.VMEM((hb, GLA_DK, GLA_DV), F32), pltpu.VMEM((chunk, kw), F32)],
        compiler_params=_params(("parallel", "parallel", "arbitrary")),
        name="gla",
    )(gp, gp, gp, la, gp, gla_norm, s0)


def _conv_gate(g, g1, g2, u, cw_ref, cb_ref, cols):
    conv = (cb_ref[0, :, cols] + cw_ref[0, 0:1, cols] * g2 + cw_ref[0, 1:2, cols] * g1
            + cw_ref[0, 2:3, cols] * g)
    return _silu(conv) * u


def _ffn_up_prompt_kernel(x_ref, halo_ref, g_ref, wg_ref, wu_ref, cw_ref, cb_ref,
                          h_ref, gtail_ref, xn_sc, *, tm, tn, tiles_per_seq):
    i = pl.program_id(0)
    j = pl.program_id(1)

    @pl.when(j == 0)
    def _():
        xn_sc[0:tm, :] = _rms(x_ref[...], g_ref[0]).astype(BF16)
        keep = jnp.where(i % tiles_per_seq == 0, 0.0, 1.0)
        xn_sc[tm:tm + BF16_ROWS, :] = (_rms(halo_ref[...], g_ref[0]) * keep).astype(BF16)

    for t in range(tn // MXU_COLS):
        cols = slice(t * MXU_COLS, (t + 1) * MXU_COLS)
        g_ext = _dot(xn_sc[...], wg_ref[0, :, cols])
        u = _dot(xn_sc[0:tm, :], wu_ref[0, :, cols])
        g = g_ext[0:tm]
        g1 = pltpu.roll(g_ext, 1, axis=0)[0:tm]
        g2 = pltpu.roll(g_ext, 2, axis=0)[0:tm]
        h_ref[:, cols] = _conv_gate(g, g1, g2, u, cw_ref, cb_ref, cols).astype(h_ref.dtype)
        gtail_ref[:, cols] = g[tm - SUBLANES:tm]


def ffn_up_prompt(x, gamma, wg, wu, conv_w, conv_b, *, layer, seq, tm, tn):
    m = x.shape[0]
    tiles_per_seq = seq // tm
    halo_blocks = tm // BF16_ROWS
    return pl.pallas_call(
        functools.partial(_ffn_up_prompt_kernel, tm=tm, tn=tn, tiles_per_seq=tiles_per_seq),
        grid=(m // tm, D_FF // tn),
        in_specs=[
            pl.BlockSpec((tm, D_MODEL), lambda i, j: (i, 0)),
            pl.BlockSpec((BF16_ROWS, D_MODEL), lambda i, j: (jnp.maximum(i * halo_blocks - 1, 0), 0)),
            pl.BlockSpec((1, 1, D_MODEL), lambda i, j: (layer, 0, 0)),
            pl.BlockSpec((1, D_MODEL, tn), lambda i, j: (layer, 0, j)),
            pl.BlockSpec((1, D_MODEL, tn), lambda i, j: (layer, 0, j)),
            pl.BlockSpec((1, CONV_W, tn), lambda i, j: (layer, 0, j)),
            pl.BlockSpec((1, 1, tn), lambda i, j: (layer, 0, j)),
        ],
        out_specs=[
            pl.BlockSpec((tm, tn), lambda i, j: (i, j)),
            pl.BlockSpec((SUBLANES, tn), lambda i, j: (i, j)),
        ],
        out_shape=[
            jax.ShapeDtypeStruct((m, D_FF), BF16),
            jax.ShapeDtypeStruct((m // tm * SUBLANES, D_FF), F32),
        ],
        scratch_shapes=[pltpu.VMEM((tm + BF16_ROWS, D_MODEL), BF16)],
        compiler_params=_params(("parallel", "arbitrary")),
        name="ffn_up_prompt",
    )(x, x, gamma, wg, wu, conv_w, conv_b)


def _ffn_up_sample_kernel(x_ref, g_ref, wg_ref, wu_ref, cw_ref, cb_ref, e1_ref, e2_ref,
                          h_ref, gout_ref, xn_sc, *, seq, tn):
    j = pl.program_id(0)

    @pl.when(j == 0)
    def _():
        xn_sc[...] = _rms(x_ref[...], g_ref[0]).astype(BF16)

    cols = slice(0, tn)
    g = _dot(xn_sc[...], wg_ref[0])
    u = _dot(xn_sc[...], wu_ref[0])
    pos = lax.broadcasted_iota(jnp.int32, (g.shape[0], 1), 0) & (seq - 1)
    g1 = jnp.where(pos >= 1, pltpu.roll(g, 1, axis=0), e1_ref[...])
    g2 = jnp.where(pos >= 2, pltpu.roll(g, 2, axis=0), e2_ref[...])
    h_ref[...] = _conv_gate(g, g1, g2, u, cw_ref, cb_ref, cols).astype(h_ref.dtype)
    gout_ref[...] = g


def ffn_up_sample(x, gamma, wg, wu, conv_w, conv_b, e1, e2, *, layer, seq, tn):
    m = x.shape[0]
    return pl.pallas_call(
        functools.partial(_ffn_up_sample_kernel, seq=seq, tn=tn),
        grid=(D_FF // tn,),
        in_specs=[
            pl.BlockSpec((m, D_MODEL), lambda j: (0, 0)),
            pl.BlockSpec((1, 1, D_MODEL), lambda j: (layer, 0, 0)),
            pl.BlockSpec((1, D_MODEL, tn), lambda j: (layer, 0, j)),
            pl.BlockSpec((1, D_MODEL, tn), lambda j: (layer, 0, j)),
            pl.BlockSpec((1, CONV_W, tn), lambda j: (layer, 0, j)),
            pl.BlockSpec((1, 1, tn), lambda j: (layer, 0, j)),
            pl.BlockSpec((m, tn), lambda j: (0, j)),
            pl.BlockSpec((m, tn), lambda j: (0, j)),
        ],
        out_specs=[
            pl.BlockSpec((m, tn), lambda j: (0, j)),
            pl.BlockSpec((m, tn), lambda j: (0, j)),
        ],
        out_shape=[
            jax.ShapeDtypeStruct((m, D_FF), BF16),
            jax.ShapeDtypeStruct((m, D_FF), F32),
        ],
        scratch_shapes=[pltpu.VMEM((m, D_MODEL), BF16)],
        compiler_params=_params(("arbitrary",)),
        name="ffn_up_sample",
    )(x, gamma, wg, wu, conv_w, conv_b, e1, e2)


def _rmsnorm_kernel(x_ref, g_ref, o_ref):
    o_ref[...] = _rms(x_ref[...], g_ref[...])


def rmsnorm(x, gamma, *, tm):
    m = x.shape[0]
    return pl.pallas_call(
        _rmsnorm_kernel,
        grid=(m // tm,),
        in_specs=[pl.BlockSpec((tm, D_MODEL), lambda i: (i, 0)),
                  pl.BlockSpec((1, D_MODEL), lambda i: (0, 0))],
        out_specs=pl.BlockSpec((tm, D_MODEL), lambda i: (i, 0)),
        out_shape=jax.ShapeDtypeStruct((m, D_MODEL), F32),
        compiler_params=_params(("parallel",)),
        name="rmsnorm",
    )(x, gamma)


def _prep_weights(w_in, b_f, w_a2, b_a, w_out, w_gate, w_up, w_down):
    o_fl = 3 * FOX_WIDTH
    o_gq = o_fl + FOX_HEADS
    o_ga = o_gq + 2 * GLA_KEY_WIDTH + 2 * GLA_WIDTH
    cols = lambda a, n: w_in[:, :, a:a + n]
    w_fl = cols(o_fl, FOX_HEADS)
    w_ga = cols(o_ga, GLA_RANK)
    pad = LANES - FOX_HEADS - GLA_RANK
    zeros = lambda *s: jnp.zeros((DEPTH,) + s, F32)
    return dict(
        w_q=cols(0, FOX_WIDTH).astype(BF16),
        w_k=cols(FOX_WIDTH, FOX_WIDTH).astype(BF16),
        w_v=cols(2 * FOX_WIDTH, FOX_WIDTH).astype(BF16),
        w_g=cols(o_gq, G_WIDTH).astype(BF16),
        w_small=jnp.concatenate([w_fl, w_ga, zeros(D_MODEL, pad)], axis=2).astype(BF16),
        b_small=jnp.concatenate([b_f, zeros(LANES - FOX_HEADS)], axis=1).reshape(DEPTH, 1, LANES),
        wa2_pad=jnp.concatenate([zeros(FOX_HEADS, GLA_KEY_WIDTH), w_a2, zeros(pad, GLA_KEY_WIDTH)],
                                axis=1).astype(BF16),
        b_a=b_a.reshape(DEPTH, 1, GLA_KEY_WIDTH),
        wf_t=jnp.pad(jnp.swapaxes(w_fl, 1, 2), ((0, 0), (0, BF16_ROWS - FOX_HEADS), (0, 0))).astype(BF16),
        bf_t=jnp.pad(b_f.reshape(DEPTH, FOX_HEADS, 1), ((0, 0), (0, BF16_ROWS - FOX_HEADS), (0, 0))),
        w_out=w_out.astype(BF16), w_gate=w_gate.astype(BF16), w_up=w_up.astype(BF16),
        w_down=w_down.astype(BF16),
    )


def _pad_rows(x, batch, t, t_pad):
    w = x.shape[-1]
    return jnp.pad(x.reshape(batch, t, w), ((0, 0), (0, t_pad - t), (0, 0))).reshape(batch * t_pad, w)


def kernel(x_prompt, x_sample, cache_k, cache_v, cache_logf, state_gla, state_conv, page_table,
           w_in, b_f, w_a2, b_a, fox_norm, gla_norm, w_out, norm_attn, norm_ffn,
           w_gate, w_up, conv_w, conv_b, w_down, norm_final):
    bp, seq, _ = x_prompt.shape
    db, t_new, _ = x_sample.shape
    n_pool = cache_k.shape[1]
    mp, ms = bp * seq, db * t_new

    xp = x_prompt.reshape(mp, D_MODEL)
    xs = x_sample.reshape(ms, D_MODEL)
    cache_k4 = cache_k.reshape(DEPTH, n_pool, PAGE * FOX_HEADS, FOX_DIM)
    cache_v4 = cache_v.reshape(DEPTH, n_pool, PAGE * FOX_HEADS, FOX_DIM)
    cache_lft = jnp.swapaxes(cache_logf, 2, 3)
    q_scale = jnp.full((1, FOX_WIDTH), FOX_DIM ** -0.5 * LOG2E, F32)
    g_scale = jnp.concatenate([jnp.full((GLA_KEY_WIDTH,), GLA_DK ** -0.5, F32),
                               jnp.ones((G_WIDTH - GLA_KEY_WIDTH,), F32)]).reshape(1, G_WIDTH)
    zero_state = jnp.zeros((bp, GLA_HEADS, GLA_DK, GLA_DV), F32)
    w = _prep_weights(w_in, b_f, w_a2, b_a, w_out, w_gate, w_up, w_down)
    conv_b3 = conv_b.reshape(DEPTH, 1, D_FF)
    norm_attn = norm_attn.reshape(DEPTH, 1, D_MODEL)
    norm_ffn = norm_ffn.reshape(DEPTH, 1, D_MODEL)
    fox_norm = fox_norm.reshape(DEPTH, 1, FOX_WIDTH)
    gla_norm = gla_norm.reshape(DEPTH, 1, GLA_WIDTH)
    small = (w["w_small"], w["b_small"], w["wa2_pad"], w["b_a"], w["wf_t"], w["bf_t"])

    def projections(xn, l, tm):
        (q,) = matmul([xn], [w["w_q"]], layer=l, out_dtypes=[BF16], tm=tm, tn=TN, n=FOX_WIDTH, scale=q_scale)
        k32, k16 = matmul([xn], [w["w_k"]], layer=l, out_dtypes=[F32, BF16], tm=tm, tn=TN, n=FOX_WIDTH)
        v32, v16 = matmul([xn], [w["w_v"]], layer=l, out_dtypes=[F32, BF16], tm=tm, tn=TN, n=FOX_WIDTH)
        (gp,) = matmul([xn], [w["w_g"]], layer=l, out_dtypes=[BF16], tm=tm, tn=2 * TN, n=G_WIDTH, scale=g_scale)
        return q, k32, k16, v32, v16, gp

    def out_and_down(fo, go, h_fn, x, l, tm):
        (x,) = matmul([fo, go], [w["w_out"], w["w_out"]], layer=l, out_dtypes=[F32], tm=tm, tn=TN,
                      n=D_MODEL, w_rows=[0, 1], res=x)
        h, extra = h_fn(x)
        (x,) = matmul([h], [w["w_down"]], layer=l, out_dtypes=[F32], tm=tm, tn=TN, n=D_MODEL, res=x)
        return x, extra

    outs = {k: [] for k in ("kp", "vp", "lp", "sp", "cp", "ks", "vs", "ls", "ss", "cs")}
    for l in range(DEPTH):
        xn, logf, la, _, cx = norm_small(xp, norm_attn, *small, layer=l, tm=TQ, seg=seq, emit_cx=True)
        q, k32, k16, v32, _, gp = projections(xn, l, TM)
        fo = fox_prompt(q, k16, cx, v32, fox_norm, layer=l, batch=bp, seq=seq)
        go, s_fin = gla(gp, la, gla_norm, zero_state, layer=l, batch=bp, seq=seq)
        ffn = lambda x: ffn_up_prompt(x, norm_ffn, w["w_gate"], w["w_up"], conv_w, conv_b3,
                                      layer=l, seq=seq, tm=TM, tn=TN)
        xp, gtail = out_and_down(fo, go, ffn, xp, l, TM)
        outs["kp"].append(k32.reshape(bp, seq, FOX_HEADS, FOX_DIM))
        outs["vp"].append(v32.reshape(bp, seq, FOX_HEADS, FOX_DIM))
        outs["lp"].append(logf.reshape(bp, seq, FOX_HEADS))
        outs["sp"].append(s_fin)
        outs["cp"].append(gtail.reshape(bp, seq // TM, SUBLANES, D_FF)[:, -1, SUBLANES - (CONV_W - 1):, :])

        xn, logf, la, c = norm_small(xs, norm_attn, *small, layer=l, tm=ms, seg=t_new, emit_cx=False)
        q, k32, k16, v32, v16, gp = projections(xn, l, ms)
        kn_pad = _pad_rows(k16, db, t_new, PAGE)
        vn_pad = _pad_rows(v16, db, t_new, PAGE)
        c_new = c.reshape(FOX_HEADS, db, t_new).transpose(1, 0, 2)
        c_new_pad = jnp.pad(c_new, ((0, 0), (0, 0), (0, PAGE - t_new)))
        fo = fox_sample(page_table, q.astype(F32), kn_pad, vn_pad, c_new_pad, fox_norm,
                        cache_k4, cache_v4, cache_lft, layer=l, pages=SAMPLE_PAGES)
        gp_pad = _pad_rows(gp, db, t_new, GLA_CHUNK)
        la_pad = _pad_rows(la, db, t_new, GLA_CHUNK)
        go_pad, s_fin = gla(gp_pad, la_pad, gla_norm, state_gla[l], layer=l, batch=db, seq=GLA_CHUNK)
        go = go_pad.reshape(db, GLA_CHUNK, GLA_WIDTH)[:, :t_new].reshape(ms, GLA_WIDTH)
        st = state_conv[l]
        zeros_f = jnp.zeros((db, t_new - 1, D_FF), F32)
        e1 = jnp.concatenate([st[:, 1:2], zeros_f], axis=1).reshape(ms, D_FF)
        e2 = jnp.concatenate([st[:, 0:2], zeros_f[:, 1:]], axis=1).reshape(ms, D_FF)
        ffn = lambda x: ffn_up_sample(x, norm_ffn, w["w_gate"], w["w_up"], conv_w, conv_b3, e1, e2,
                                      layer=l, seq=t_new, tn=TN)
        xs, g_all = out_and_down(fo.astype(BF16), go, ffn, xs, l, ms)
        outs["ks"].append(k32.reshape(db, t_new, FOX_HEADS, FOX_DIM))
        outs["vs"].append(v32.reshape(db, t_new, FOX_HEADS, FOX_DIM))
        outs["ls"].append(logf.reshape(db, t_new, FOX_HEADS))
        outs["ss"].append(s_fin)
        outs["cs"].append(g_all.reshape(db, t_new, D_FF)[:, t_new - (CONV_W - 1):, :])

    g_fin = norm_final.reshape(1, D_MODEL)
    y_prompt = rmsnorm(xp, g_fin, tm=TM).reshape(bp, seq, D_MODEL)
    y_sample = rmsnorm(xs, g_fin, tm=ms).reshape(db, t_new, D_MODEL)
    stack = lambda key: jnp.stack(outs[key])
    return (y_prompt, y_sample,
            stack("kp"), stack("vp"), stack("lp"), stack("sp"), stack("cp"),
            stack("ks"), stack("vs"), stack("ls"), stack("ss"), stack("cs"))
```

```python
import functools
import math

import jax
import jax.numpy as jnp
from jax import lax
from jax.experimental import pallas as pl
from jax.experimental.pallas import tpu as pltpu

F32 = jnp.float32
BF16 = jnp.bfloat16

D_MODEL = 2048
DEPTH = 4
PAGE = 128
FOX_HEADS = 8
FOX_DIM = 128
FOX_WIDTH = FOX_HEADS * FOX_DIM
GLA_HEADS = 4
GLA_DK = 128
GLA_DV = 256
GLA_KEY_WIDTH = GLA_HEADS * GLA_DK
GLA_WIDTH = GLA_HEADS * GLA_DV
GLA_RANK = 16
GLA_NORMALIZER = 16.0
D_FF = 5632
CONV_W = 3
EPS = 1e-6
LOG2E = math.log2(math.e)

LANES = 128
SUBLANES = 8
BF16_ROWS = 16
MXU_COLS = 256
VMEM_LIMIT = 56 * 1024 * 1024

GLA_CHUNK = 128
GLA_SUB = 8
GLA_HEAD_BLOCK = 4
NEG_BIG = -1e30

TM = 1024
TN = 512
TQ = 512
SAMPLE_PAGES = 8

G_GQ, G_GK, G_GV, G_GR = 0, 512, 1024, 2048
G_WIDTH = 3072


def _params(sem):
    return pltpu.CompilerParams(dimension_semantics=sem, vmem_limit_bytes=VMEM_LIMIT)


def _log_sigmoid(x):
    return jnp.minimum(x, 0.0) - jnp.log(1.0 + jnp.exp(-jnp.abs(x)))


def _silu(x):
    return x * (1.0 / (1.0 + jnp.exp(-x)))


def _rms(x, g):
    return x * lax.rsqrt(jnp.mean(x * x, axis=-1, keepdims=True) + EPS) * g


def _dot(a, b):
    return jnp.dot(a, b, preferred_element_type=F32)


def _dot_nt(a, b):
    return lax.dot_general(a, b, (((1,), (1,)), ((), ())), preferred_element_type=F32)


def _dot_tn(a, b):
    return lax.dot_general(a, b, (((0,), (0,)), ((), ())), preferred_element_type=F32)


def _split3(x):
    hi = x.astype(BF16).astype(F32)
    r1 = x - hi
    mid = r1.astype(BF16).astype(F32)
    lo = (r1 - mid).astype(BF16).astype(F32)
    return hi, mid, lo


def _norm_small_kernel(x_ref, g_ref, ws_ref, bs_ref, wa2_ref, ba_ref, wft_ref, bft_ref,
                       xn_ref, logf_ref, la_ref, c_ref, *rest, tm, seg, emit_cx):
    if emit_cx:
        cx_ref, carry_ref = rest
    else:
        (carry_ref,) = rest
    i = pl.program_id(0)
    xn = _rms(x_ref[...], g_ref[0]).astype(BF16)
    xn_ref[...] = xn
    zs = _dot(xn, ws_ref[0])
    logf_ref[...] = _log_sigmoid(zs + bs_ref[0])[:, :FOX_HEADS]
    la_pre = _dot(zs.astype(BF16), wa2_ref[0]) + ba_ref[0]
    la_ref[...] = _log_sigmoid(la_pre) * (1.0 / GLA_NORMALIZER)

    lft = _log_sigmoid(_dot_nt(wft_ref[0], xn) + bft_ref[0])[:FOX_HEADS]
    lane = lax.broadcasted_iota(jnp.int32, (FOX_HEADS, tm), 1)
    pos = lane & (seg - 1) if seg < tm else lane
    c = lft
    d = 1
    while d < min(seg, tm):
        c = c + jnp.where(pos >= d, pltpu.roll(c, d, axis=1), 0.0)
        d *= 2
    if seg > tm:
        tiles_per_seq = seg // tm

        @pl.when(i % tiles_per_seq == 0)
        def _():
            carry_ref[...] = jnp.zeros_like(carry_ref)

        c = c + carry_ref[:, 0:1]
        carry_ref[...] = jnp.broadcast_to(c[:, tm - 1:tm], carry_ref.shape)
    c_ref[0] = c

    if emit_cx:
        c2 = c * (-LOG2E)
        stacked = jnp.concatenate(
            [c2, c2, c2, jnp.zeros((LANES - 3 * FOX_HEADS, tm), F32)], axis=0)
        hi, mid, lo = _split3(stacked)
        rowid = lax.broadcasted_iota(jnp.int32, stacked.shape, 0)
        pieces = jnp.where(rowid < FOX_HEADS, hi, jnp.where(rowid < 2 * FOX_HEADS, mid, lo))
        cx_ref[...] = pieces.T.astype(BF16)


def norm_small(x, gamma, w_small, b_small, wa2_pad, b_a, wf_t, bf_t, *, layer, tm, seg, emit_cx):
    m = x.shape[0]
    n_tiles = m // tm
    if seg >= tm:
        tps = seg // tm
        c_shape = (m // seg, FOX_HEADS, seg)
        c_map = lambda i: (i // tps, 0, i % tps)
    else:
        c_shape = (n_tiles, FOX_HEADS, tm)
        c_map = lambda i: (i, 0, 0)
    lay = lambda i: (layer, 0, 0)
    out_specs = [
        pl.BlockSpec((tm, D_MODEL), lambda i: (i, 0)),
        pl.BlockSpec((tm, FOX_HEADS), lambda i: (i, 0)),
        pl.BlockSpec((tm, GLA_KEY_WIDTH), lambda i: (i, 0)),
        pl.BlockSpec((1, FOX_HEADS, tm), c_map),
    ]
    out_shape = [
        jax.ShapeDtypeStruct((m, D_MODEL), BF16),
        jax.ShapeDtypeStruct((m, FOX_HEADS), F32),
        jax.ShapeDtypeStruct((m, GLA_KEY_WIDTH), F32),
        jax.ShapeDtypeStruct(c_shape, F32),
    ]
    if emit_cx:
        out_specs.append(pl.BlockSpec((tm, LANES), lambda i: (i, 0)))
        out_shape.append(jax.ShapeDtypeStruct((m, LANES), BF16))
    return pl.pallas_call(
        functools.partial(_norm_small_kernel, tm=tm, seg=seg, emit_cx=emit_cx),
        grid=(n_tiles,),
        in_specs=[
            pl.BlockSpec((tm, D_MODEL), lambda i: (i, 0)),
            pl.BlockSpec((1, 1, D_MODEL), lay),
            pl.BlockSpec((1, D_MODEL, LANES), lay),
            pl.BlockSpec((1, 1, LANES), lay),
            pl.BlockSpec((1, LANES, GLA_KEY_WIDTH), lay),
            pl.BlockSpec((1, 1, GLA_KEY_WIDTH), lay),
            pl.BlockSpec((1, BF16_ROWS, D_MODEL), lay),
            pl.BlockSpec((1, BF16_ROWS, 1), lay),
        ],
        out_specs=out_specs,
        out_shape=out_shape,
        scratch_shapes=[pltpu.VMEM((FOX_HEADS, LANES), F32)],
        compiler_params=_params(("arbitrary",)),
        name="norm_small",
    )(x, gamma, w_small, b_small, wa2_pad, b_a, wf_t, bf_t)


def _mm_kernel(*refs, n_a, has_scale, has_res, has_stack_in, n_out):
    a_refs = refs[:n_a]
    w_refs = refs[n_a:2 * n_a]
    pos = 2 * n_a
    acc = _dot(a_refs[0][...], w_refs[0][0])
    for a_ref, w_ref in zip(a_refs[1:], w_refs[1:]):
        acc = acc + _dot(a_ref[...], w_ref[0])
    if has_scale:
        acc = acc * refs[pos][...]
        pos += 1
    if has_res:
        acc = refs[pos][...] + acc
        pos += 1
    if has_stack_in:
        pos += 1
    for o_ref in refs[pos:pos + n_out]:
        o_ref[...] = acc.astype(o_ref.dtype)


def matmul(a_list, w_list, *, layer, out_dtypes, tm, tn, n, w_rows=None, scale=None, res=None,
           stacked=None):
    m = a_list[0].shape[0]
    n_i = m // tm
    w_rows = w_rows or [0] * len(a_list)
    in_specs = [pl.BlockSpec((tm, a.shape[1]), lambda i, j: (i, 0)) for a in a_list]
    for a, w, r in zip(a_list, w_list, w_rows):
        in_specs.append(pl.BlockSpec((1, a.shape[1], tn), lambda i, j, r=r: (layer, r, j)))
    args = list(a_list) + list(w_list)
    if scale is not None:
        in_specs.append(pl.BlockSpec((1, tn), lambda i, j: (0, j)))
        args.append(scale)
    if res is not None:
        in_specs.append(pl.BlockSpec((tm, tn), lambda i, j: (i, j)))
        args.append(res)
    out_specs = [pl.BlockSpec((tm, tn), lambda i, j: (i, j)) for _ in out_dtypes]
    out_shape = [jax.ShapeDtypeStruct((m, n), dt) for dt in out_dtypes]
    aliases = {}
    has_stack_in = False
    if stacked is not None:
        out_specs[0] = pl.BlockSpec((tm, tn), lambda i, j: (layer * n_i + i, j))
        out_shape[0] = jax.ShapeDtypeStruct((DEPTH * m, n), out_dtypes[0])
        if stacked[0] is not None:
            has_stack_in = True
            in_specs.append(pl.BlockSpec(memory_space=pl.ANY))
            aliases = {len(args): 0}
            args.append(stacked[0])
    outs = pl.pallas_call(
        functools.partial(_mm_kernel, n_a=len(a_list), has_scale=scale is not None,
                          has_res=res is not None, has_stack_in=has_stack_in, n_out=len(out_dtypes)),
        grid=(n_i, n // tn),
        in_specs=in_specs,
        out_specs=out_specs,
        out_shape=out_shape,
        input_output_aliases=aliases,
        compiler_params=_params(("parallel", "arbitrary")),
        name="matmul",
    )(*args)
    return outs


def _fox_prompt_kernel(q_ref, k_ref, cx_ref, v_ref, gn_ref, o_ref, kx_sc, vt_sc, *, seq, tq):
    h = pl.program_id(1)
    kx_sc[:, 0:FOX_DIM] = k_ref[...]
    kx_sc[:, FOX_DIM:2 * FOX_DIM] = cx_ref[...]
    vt_sc[0:FOX_DIM, :] = v_ref[...].astype(F32).T.astype(BF16)
    vt_sc[FOX_DIM:FOX_DIM + BF16_ROWS, :] = jnp.ones((BF16_ROWS, seq), BF16)

    lane = lax.broadcasted_iota(jnp.int32, (tq, FOX_DIM), 1)
    pick = (lane == h) | (lane == h + FOX_HEADS) | (lane == h + 2 * FOX_HEADS)
    onehot = jnp.where(pick, 1.0, 0.0).astype(BF16)
    key_i = lax.broadcasted_iota(jnp.int32, (tq, tq), 0)
    qry_i = lax.broadcasted_iota(jnp.int32, (tq, tq), 1)
    causal = key_i <= qry_i

    for qi in range(seq // tq):
        lo, hi = qi * tq, (qi + 1) * tq
        qx = jnp.concatenate([q_ref[lo:hi, :], onehot], axis=1)
        s_diag = jnp.where(causal, _dot_nt(kx_sc[lo:hi, :], qx), NEG_BIG)
        m = jnp.max(s_diag, axis=0, keepdims=True)
        if qi > 0:
            s_off = _dot_nt(kx_sc[0:lo, :], qx)
            m = jnp.maximum(m, jnp.max(s_off, axis=0, keepdims=True))
            p_off = jnp.exp2(s_off - m).astype(BF16)
            ot = _dot(vt_sc[:, 0:lo], p_off)
        p_diag = jnp.exp2(s_diag - m).astype(BF16)
        ot_d = _dot(vt_sc[:, lo:hi], p_diag)
        ot = ot + ot_d if qi > 0 else ot_d
        o = ot[0:FOX_DIM, :] * (1.0 / ot[FOX_DIM:FOX_DIM + 1, :])
        y = o * lax.rsqrt(jnp.mean(o * o, axis=0, keepdims=True) + EPS)
        o_ref[lo:hi, :] = (y.T * gn_ref[0]).astype(o_ref.dtype)


def fox_prompt(q, k16, cx, v16, fox_norm, *, layer, batch, seq):
    m = batch * seq
    return pl.pallas_call(
        functools.partial(_fox_prompt_kernel, seq=seq, tq=TQ),
        grid=(batch, FOX_HEADS),
        in_specs=[
            pl.BlockSpec((seq, FOX_DIM), lambda b, h: (b, h)),
            pl.BlockSpec((seq, FOX_DIM), lambda b, h: (b, h)),
            pl.BlockSpec((seq, LANES), lambda b, h: (b, 0)),
            pl.BlockSpec((seq, FOX_DIM), lambda b, h: (b, h)),
            pl.BlockSpec((1, 1, FOX_DIM), lambda b, h: (layer, 0, h)),
        ],
        out_specs=pl.BlockSpec((seq, FOX_DIM), lambda b, h: (b, h)),
        out_shape=jax.ShapeDtypeStruct((m, FOX_WIDTH), BF16),
        scratch_shapes=[pltpu.VMEM((seq, 2 * FOX_DIM), BF16),
                        pltpu.VMEM((FOX_DIM + BF16_ROWS, seq), BF16)],
        compiler_params=_params(("parallel", "arbitrary")),
        name="fox_prompt",
    )(q, k16, cx, v16, fox_norm)


def _fox_sample_kernel(pt_ref, q_ref, kn_ref, vn_ref, cn_ref, gn_ref, *rest, pages, t_new):
    k_refs = rest[:pages]
    v_refs = rest[pages:2 * pages]
    lf_refs = rest[2 * pages:3 * pages]
    o_ref, m_sc, l_sc, acc_sc, carry_sc = rest[3 * pages:]
    del pt_ref
    step = pl.program_id(1)
    n_steps = pl.num_programs(1)
    rows = t_new

    @pl.when(step == 0)
    def _():
        m_sc[...] = jnp.full_like(m_sc, NEG_BIG)
        l_sc[...] = jnp.zeros_like(l_sc)
        acc_sc[...] = jnp.zeros_like(acc_sc)
        carry_sc[...] = jnp.zeros_like(carry_sc)

    flat = PAGE * FOX_HEADS
    lane = lax.broadcasted_iota(jnp.int32, (pages, flat), 1)
    lf = jnp.concatenate([lf_refs[p][0, 0] for p in range(pages)], axis=0)
    suf, tot = lf, lf
    d = FOX_HEADS
    while d < flat:
        suf = suf + jnp.where(lane < flat - d, pltpu.roll(suf, flat - d, axis=1), 0.0)
        tot = tot + pltpu.roll(tot, d, axis=1)
        d *= 2
    run = carry_sc[0:1, :]
    bias = [None] * pages
    for p in range(pages - 1, -1, -1):
        bias[p] = ((suf[p:p + 1, :] - lf[p:p + 1, :]) + run) * LOG2E
        run = run + tot[p:p + 1, :]
    carry_sc[0:1, :] = run

    n_rows = FOX_HEADS * rows
    q_flat = jnp.concatenate(
        [q_ref[:, h * FOX_DIM:(h + 1) * FOX_DIM] for h in range(FOX_HEADS)], axis=0).astype(BF16)
    zeros = jnp.zeros_like(q_flat)
    q_pair = jnp.concatenate([jnp.concatenate([q_flat, zeros], axis=1),
                              jnp.concatenate([zeros, q_flat], axis=1)], axis=0)
    row_i = lax.broadcasted_iota(jnp.int32, (n_rows, flat), 0)
    col_i = lax.broadcasted_iota(jnp.int32, (n_rows, flat), 1)
    own = (row_i >> (rows.bit_length() - 1)) == (col_i & (FOX_HEADS - 1))

    def attend(s, v_all):
        m_old = m_sc[...]
        m_new = jnp.maximum(m_old, jnp.max(s, axis=-1, keepdims=True))
        alpha = jnp.exp2(m_old - m_new)
        pr = jnp.exp2(s - m_new)
        l_sc[...] = alpha * l_sc[...] + jnp.sum(pr, axis=-1, keepdims=True)
        m_sc[...] = m_new
        acc_sc[...] = alpha * acc_sc[...] + _dot(pr.astype(BF16), v_all)

    s_parts = [None] * pages
    for a in range(0, pages, 2):
        kk = jnp.concatenate([k_refs[a][0, 0], k_refs[a + 1][0, 0]], axis=1).astype(BF16)
        s2 = _dot_nt(q_pair, kk)
        s_parts[a] = jnp.where(own, s2[0:n_rows] + bias[a], NEG_BIG)
        s_parts[a + 1] = jnp.where(own, s2[n_rows:2 * n_rows] + bias[a + 1], NEG_BIG)
    attend(jnp.concatenate(s_parts, axis=1),
           jnp.concatenate([v_refs[p][0, 0].astype(BF16) for p in range(pages)], axis=0))

    @pl.when(step == n_steps - 1)
    def _():
        valid = own & ((col_i >> (FOX_HEADS.bit_length() - 1)) <= (row_i & (rows - 1)))
        s_new = _dot_nt(q_flat, kn_ref[...]) - cn_ref[0] * LOG2E
        attend(jnp.where(valid, s_new, NEG_BIG), vn_ref[...])
        for h in range(FOX_HEADS):
            cols = slice(h * FOX_DIM, (h + 1) * FOX_DIM)
            sl = slice(h * rows, (h + 1) * rows)
            o = acc_sc[sl, :] / l_sc[sl, :]
            o_ref[:, cols] = _rms(o, gn_ref[0, :, cols])


def fox_sample(page_table, q, k_new_page, v_new_page, c_new_flat, fox_norm, cache_k4, cache_v4,
               cache_lf_flat, *, layer, pages):
    db, n_pages = page_table.shape
    t_new = q.shape[0] // db
    assert t_new == SUBLANES and pages % 2 == 0 and n_pages % pages == 0
    n_steps = n_pages // pages
    flat = PAGE * FOX_HEADS
    pt_flat = page_table.reshape(-1)

    def page_map(p):
        def index(b, s, pt):
            return (layer, pt[b * n_pages + (n_steps - 1 - s) * pages + p], 0, 0)
        return index

    in_specs = [
        pl.BlockSpec((t_new, FOX_WIDTH), lambda b, s, pt: (b, 0)),
        pl.BlockSpec((flat, FOX_DIM), lambda b, s, pt: (b, 0)),
        pl.BlockSpec((flat, FOX_DIM), lambda b, s, pt: (b, 0)),
        pl.BlockSpec((1, 1, flat), lambda b, s, pt: (b, 0, 0)),
        pl.BlockSpec((1, 1, FOX_WIDTH), lambda b, s, pt: (layer, 0, 0)),
    ]
    in_specs += [pl.BlockSpec((1, 1, flat, FOX_DIM), page_map(p)) for p in range(pages)]
    in_specs += [pl.BlockSpec((1, 1, flat, FOX_DIM), page_map(p)) for p in range(pages)]
    in_specs += [pl.BlockSpec((1, 1, 1, flat), page_map(p)) for p in range(pages)]
    grid_spec = pltpu.PrefetchScalarGridSpec(
        num_scalar_prefetch=1,
        grid=(db, n_steps),
        in_specs=in_specs,
        out_specs=pl.BlockSpec((t_new, FOX_WIDTH), lambda b, s, pt: (b, 0)),
        scratch_shapes=[pltpu.VMEM((FOX_HEADS * t_new, 1), F32),
                        pltpu.VMEM((FOX_HEADS * t_new, 1), F32),
                        pltpu.VMEM((FOX_HEADS * t_new, FOX_DIM), F32),
                        pltpu.VMEM((SUBLANES, flat), F32)],
    )
    return pl.pallas_call(
        functools.partial(_fox_sample_kernel, pages=pages, t_new=t_new),
        grid_spec=grid_spec,
        out_shape=jax.ShapeDtypeStruct((db * t_new, FOX_WIDTH), F32),
        compiler_params=_params(("parallel", "arbitrary")),
        name="fox_sample",
    )(pt_flat, q, k_new_page, v_new_page, c_new_flat, fox_norm,
      *([cache_k4] * pages), *([cache_v4] * pages), *([cache_lf_flat] * pages))


def _gla_kernel(q_ref, k_ref, v_ref, la_ref, r_ref, gn_ref, s0_ref, o_ref, sfin_ref,
                s_sc, b_sc, *, chunk, sub, hb):
    ci = pl.program_id(2)
    n_chunks = pl.num_programs(2)
    heads = range(hb)
    kcols = lambda h: slice(h * GLA_DK, (h + 1) * GLA_DK)
    vcols = lambda h: slice(h * GLA_DV, (h + 1) * GLA_DV)

    @pl.when(ci == 0)
    def _():
        s_sc[...] = s0_ref[0]

    q = [q_ref[:, kcols(h)].astype(F32) for h in heads]
    k = [k_ref[:, kcols(h)].astype(F32) for h in heads]
    v = [v_ref[:, vcols(h)] for h in heads]
    rowi = lax.broadcasted_iota(jnp.int32, (chunk, chunk), 0)
    coli = lax.broadcasted_iota(jnp.int32, (chunk, chunk), 1)
    row = lax.broadcasted_iota(jnp.int32, (chunk, 1), 0)

    tri = jnp.where(rowi >= coli, 1.0, 0.0).astype(BF16)
    hi, mid, lo = _split3(la_ref[...])
    b_all = _dot(tri, hi.astype(BF16)) + _dot(tri, mid.astype(BF16)) + _dot(tri, lo.astype(BF16))
    b_sc[...] = b_all
    b = [b_all[:, kcols(h)] for h in heads]
    b_last = [b_sc[chunk - 1:chunk, kcols(h)] for h in heads]

    ones = jnp.ones((GLA_DK, chunk), BF16)
    diff = rowi - coli
    sub_shift = sub.bit_length() - 1
    a = [jnp.zeros((chunk, chunk), F32) for _ in heads]
    for d in range(sub):
        for h in heads:
            k_s = k[h] if d == 0 else pltpu.roll(k[h], d, axis=0)
            b_s = b[h] if d == 0 else pltpu.roll(b[h], d, axis=0)
            prod = q[h] * k_s * jnp.exp(b[h] - b_s)
            band = _dot(prod.astype(BF16), ones)
            a[h] = a[h] + jnp.where(diff == d, band, 0.0)
    in_sub = (rowi >> sub_shift) == (coli >> sub_shift)
    a = [jnp.where(in_sub, a[h], 0.0) for h in heads]

    m = sub
    while m < chunk:
        groups = chunk // (2 * m)
        odd = (row & m) != 0
        shift = (2 * m).bit_length() - 1
        same = (rowi >> shift) == (coli >> shift)
        for h in heads:
            ref_rows = [jnp.broadcast_to(b_sc[g * 2 * m + m - 1:g * 2 * m + m, kcols(h)], (2 * m, GLA_DK))
                        for g in range(groups)]
            ref_b = ref_rows[0] if groups == 1 else jnp.concatenate(ref_rows, axis=0)
            e = jnp.exp(-jnp.abs(b[h] - ref_b))
            qm = jnp.where(odd, q[h] * e, 0.0).astype(BF16)
            km = jnp.where(odd, 0.0, k[h] * e).astype(BF16)
            a[h] = a[h] + jnp.where(same, _dot_nt(qm, km), 0.0)
        m *= 2

    s_old = [s_sc[h] for h in heads]
    o = [_dot((q[h] * jnp.exp(b[h])).astype(BF16), s_old[h].astype(BF16))
         + _dot(a[h].astype(BF16), v[h]) for h in heads]

    s_new = []
    for h in heads:
        kl = (k[h] * jnp.exp(b_last[h] - b[h])).astype(BF16)
        decay = jnp.broadcast_to(jnp.exp(b_last[h]), (GLA_DK, GLA_DK)).T
        decay = jnp.concatenate([decay, decay], axis=1)
        s_new.append(decay * s_old[h] + _dot_tn(kl, v[h]))
        s_sc[h] = s_new[h]

    @pl.when(ci == n_chunks - 1)
    def _():
        for h in heads:
            sfin_ref[0, h] = s_new[h]

    for h in heads:
        r = r_ref[:, vcols(h)].astype(F32)
        o_ref[:, vcols(h)] = (_rms(o[h], gn_ref[0, :, vcols(h)]) * _silu(r)).astype(o_ref.dtype)


def gla(gp, la, gla_norm, s0, *, layer, batch, seq):
    m = batch * seq
    chunk = GLA_CHUNK
    hb = GLA_HEAD_BLOCK
    kw, vw = hb * GLA_DK, hb * GLA_DV
    n_chunks = seq // chunk
    rows = lambda b, h, c: b * n_chunks + c
    return pl.pallas_call(
        functools.partial(_gla_kernel, chunk=chunk, sub=GLA_SUB, hb=hb),
        grid=(batch, GLA_HEADS // hb, n_chunks),
        in_specs=[
            pl.BlockSpec((chunk, kw), lambda b, h, c: (rows(b, h, c), G_GQ // kw + h)),
            pl.BlockSpec((chunk, kw), lambda b, h, c: (rows(b, h, c), G_GK // kw + h)),
            pl.BlockSpec((chunk, vw), lambda b, h, c: (rows(b, h, c), G_GV // vw + h)),
            pl.BlockSpec((chunk, kw), lambda b, h, c: (rows(b, h, c), h)),
            pl.BlockSpec((chunk, vw), lambda b, h, c: (rows(b, h, c), G_GR // vw + h)),
            pl.BlockSpec((1, 1, vw), lambda b, h, c: (layer, 0, h)),
            pl.BlockSpec((1, hb, GLA_DK, GLA_DV), lambda b, h, c: (b, h, 0, 0)),
        ],
        out_specs=[
            pl.BlockSpec((chunk, vw), lambda b, h, c: (rows(b, h, c), h)),
            pl.BlockSpec((1, hb, GLA_DK, GLA_DV), lambda b, h, c: (b, h, 0, 0)),
        ],
        out_shape=[
            jax.ShapeDtypeStruct((m, GLA_WIDTH), BF16),
            jax.ShapeDtypeStruct((batch, GLA_HEADS, GLA_DK, GLA_DV), F32),
        ],
        scratch_shapes=[pltpu.VMEM((hb, GLA_DK, GLA_DV), F32), pltpu.VMEM((chunk, kw), F32)],
        compiler_params=_params(("parallel", "parallel", "arbitrary")),
        name="gla",
    )(gp, gp, gp, la, gp, gla_norm, s0)


def _conv_gate(g, g1, g2, u, cw_ref, cb_ref, cols):
    conv = (cb_ref[0, :, cols] + cw_ref[0, 0:1, cols] * g2 + cw_ref[0, 1:2, cols] * g1
            + cw_ref[0, 2:3, cols] * g)
    return _silu(conv) * u


def _ffn_up_prompt_kernel(x_ref, halo_ref, g_ref, wg_ref, wu_ref, cw_ref, cb_ref,
                          h_ref, gtail_ref, xn_sc, *, tm, tn, tiles_per_seq):
    i = pl.program_id(0)
    j = pl.program_id(1)

    @pl.when(j == 0)
    def _():
        xn_sc[0:tm, :] = _rms(x_ref[...], g_ref[0]).astype(BF16)
        keep = jnp.where(i % tiles_per_seq == 0, 0.0, 1.0)
        xn_sc[tm:tm + BF16_ROWS, :] = (_rms(halo_ref[...], g_ref[0]) * keep).astype(BF16)

    for t in range(tn // MXU_COLS):
        cols = slice(t * MXU_COLS, (t + 1) * MXU_COLS)
        g_ext = _dot(xn_sc[...], wg_ref[0, :, cols])
        u = _dot(xn_sc[0:tm, :], wu_ref[0, :, cols])
        g = g_ext[0:tm]
        g1 = pltpu.roll(g_ext, 1, axis=0)[0:tm]
        g2 = pltpu.roll(g_ext, 2, axis=0)[0:tm]
        h_ref[:, cols] = _conv_gate(g, g1, g2, u, cw_ref, cb_ref, cols).astype(h_ref.dtype)
        gtail_ref[:, cols] = g[tm - SUBLANES:tm]


def ffn_up_prompt(x, gamma, wg, wu, conv_w, conv_b, *, layer, seq, tm, tn):
    m = x.shape[0]
    tiles_per_seq = seq // tm
    halo_blocks = tm // BF16_ROWS
    return pl.pallas_call(
        functools.partial(_ffn_up_prompt_kernel, tm=tm, tn=tn, tiles_per_seq=tiles_per_seq),
        grid=(m // tm, D_FF // tn),
        in_specs=[
            pl.BlockSpec((tm, D_MODEL), lambda i, j: (i, 0)),
            pl.BlockSpec((BF16_ROWS, D_MODEL), lambda i, j: (jnp.maximum(i * halo_blocks - 1, 0), 0)),
            pl.BlockSpec((1, 1, D_MODEL), lambda i, j: (layer, 0, 0)),
            pl.BlockSpec((1, D_MODEL, tn), lambda i, j: (layer, 0, j)),
            pl.BlockSpec((1, D_MODEL, tn), lambda i, j: (layer, 0, j)),
            pl.BlockSpec((1, CONV_W, tn), lambda i, j: (layer, 0, j)),
            pl.BlockSpec((1, 1, tn), lambda i, j: (layer, 0, j)),
        ],
        out_specs=[
            pl.BlockSpec((tm, tn), lambda i, j: (i, j)),
            pl.BlockSpec((SUBLANES, tn), lambda i, j: (i, j)),
        ],
        out_shape=[
            jax.ShapeDtypeStruct((m, D_FF), BF16),
            jax.ShapeDtypeStruct((m // tm * SUBLANES, D_FF), F32),
        ],
        scratch_shapes=[pltpu.VMEM((tm + BF16_ROWS, D_MODEL), BF16)],
        compiler_params=_params(("parallel", "arbitrary")),
        name="ffn_up_prompt",
    )(x, x, gamma, wg, wu, conv_w, conv_b)


def _ffn_up_sample_kernel(x_ref, g_ref, wg_ref, wu_ref, cw_ref, cb_ref, e1_ref, e2_ref,
                          h_ref, gout_ref, xn_sc, *, seq, tn):
    j = pl.program_id(0)

    @pl.when(j == 0)
    def _():
        xn_sc[...] = _rms(x_ref[...], g_ref[0]).astype(BF16)

    cols = slice(0, tn)
    g = _dot(xn_sc[...], wg_ref[0])
    u = _dot(xn_sc[...], wu_ref[0])
    pos = lax.broadcasted_iota(jnp.int32, (g.shape[0], 1), 0) & (seq - 1)
    g1 = jnp.where(pos >= 1, pltpu.roll(g, 1, axis=0), e1_ref[...])
    g2 = jnp.where(pos >= 2, pltpu.roll(g, 2, axis=0), e2_ref[...])
    h_ref[...] = _conv_gate(g, g1, g2, u, cw_ref, cb_ref, cols).astype(h_ref.dtype)
    gout_ref[...] = g


def ffn_up_sample(x, gamma, wg, wu, conv_w, conv_b, e1, e2, *, layer, seq, tn):
    m = x.shape[0]
    return pl.pallas_call(
        functools.partial(_ffn_up_sample_kernel, seq=seq, tn=tn),
        grid=(D_FF // tn,),
        in_specs=[
            pl.BlockSpec((m, D_MODEL), lambda j: (0, 0)),
            pl.BlockSpec((1, 1, D_MODEL), lambda j: (layer, 0, 0)),
            pl.BlockSpec((1, D_MODEL, tn), lambda j: (layer, 0, j)),
            pl.BlockSpec((1, D_MODEL, tn), lambda j: (layer, 0, j)),
            pl.BlockSpec((1, CONV_W, tn), lambda j: (layer, 0, j)),
            pl.BlockSpec((1, 1, tn), lambda j: (layer, 0, j)),
            pl.BlockSpec((m, tn), lambda j: (0, j)),
            pl.BlockSpec((m, tn), lambda j: (0, j)),
        ],
        out_specs=[
            pl.BlockSpec((m, tn), lambda j: (0, j)),
            pl.BlockSpec((m, tn), lambda j: (0, j)),
        ],
        out_shape=[
            jax.ShapeDtypeStruct((m, D_FF), BF16),
            jax.ShapeDtypeStruct((m, D_FF), F32),
        ],
        scratch_shapes=[pltpu.VMEM((m, D_MODEL), BF16)],
        compiler_params=_params(("arbitrary",)),
        name="ffn_up_sample",
    )(x, gamma, wg, wu, conv_w, conv_b, e1, e2)


def _rmsnorm_kernel(x_ref, g_ref, o_ref):
    o_ref[...] = _rms(x_ref[...], g_ref[...])


def rmsnorm(x, gamma, *, tm):
    m = x.shape[0]
    return pl.pallas_call(
        _rmsnorm_kernel,
        grid=(m // tm,),
        in_specs=[pl.BlockSpec((tm, D_MODEL), lambda i: (i, 0)),
                  pl.BlockSpec((1, D_MODEL), lambda i: (0, 0))],
        out_specs=pl.BlockSpec((tm, D_MODEL), lambda i: (i, 0)),
        out_shape=jax.ShapeDtypeStruct((m, D_MODEL), F32),
        compiler_params=_params(("parallel",)),
        name="rmsnorm",
    )(x, gamma)


def _prep_weights(w_in, b_f, w_a2, b_a, w_out, w_gate, w_up, w_down):
    o_fl = 3 * FOX_WIDTH
    o_gq = o_fl + FOX_HEADS
    o_ga = o_gq + 2 * GLA_KEY_WIDTH + 2 * GLA_WIDTH
    cols = lambda a, n: w_in[:, :, a:a + n]
    w_fl = cols(o_fl, FOX_HEADS)
    w_ga = cols(o_ga, GLA_RANK)
    pad = LANES - FOX_HEADS - GLA_RANK
    zeros = lambda *s: jnp.zeros((DEPTH,) + s, F32)
    return dict(
        w_q=cols(0, FOX_WIDTH).astype(BF16),
        w_k=cols(FOX_WIDTH, FOX_WIDTH).astype(BF16),
        w_v=cols(2 * FOX_WIDTH, FOX_WIDTH).astype(BF16),
        w_g=cols(o_gq, G_WIDTH).astype(BF16),
        w_small=jnp.concatenate([w_fl, w_ga, zeros(D_MODEL, pad)], axis=2).astype(BF16),
        b_small=jnp.concatenate([b_f, zeros(LANES - FOX_HEADS)], axis=1).reshape(DEPTH, 1, LANES),
        wa2_pad=jnp.concatenate([zeros(FOX_HEADS, GLA_KEY_WIDTH), w_a2, zeros(pad, GLA_KEY_WIDTH)],
                                axis=1).astype(BF16),
        b_a=b_a.reshape(DEPTH, 1, GLA_KEY_WIDTH),
        wf_t=jnp.pad(jnp.swapaxes(w_fl, 1, 2), ((0, 0), (0, BF16_ROWS - FOX_HEADS), (0, 0))).astype(BF16),
        bf_t=jnp.pad(b_f.reshape(DEPTH, FOX_HEADS, 1), ((0, 0), (0, BF16_ROWS - FOX_HEADS), (0, 0))),
        w_out=w_out.astype(BF16), w_gate=w_gate.astype(BF16), w_up=w_up.astype(BF16),
        w_down=w_down.astype(BF16),
    )


def _pad_rows(x, batch, t, t_pad):
    w = x.shape[-1]
    return jnp.pad(x.reshape(batch, t, w), ((0, 0), (0, t_pad - t), (0, 0))).reshape(batch * t_pad, w)


def kernel(x_prompt, x_sample, cache_k, cache_v, cache_logf, state_gla, state_conv, page_table,
           w_in, b_f, w_a2, b_a, fox_norm, gla_norm, w_out, norm_attn, norm_ffn,
           w_gate, w_up, conv_w, conv_b, w_down, norm_final):
    bp, seq, _ = x_prompt.shape
    db, t_new, _ = x_sample.shape
    n_pool = cache_k.shape[1]
    mp, ms = bp * seq, db * t_new

    xp = x_prompt.reshape(mp, D_MODEL)
    xs = x_sample.reshape(ms, D_MODEL)
    cache_k4 = cache_k.reshape(DEPTH, n_pool, PAGE * FOX_HEADS, FOX_DIM)
    cache_v4 = cache_v.reshape(DEPTH, n_pool, PAGE * FOX_HEADS, FOX_DIM)
    cache_lf_flat = cache_logf.reshape(DEPTH, n_pool, 1, PAGE * FOX_HEADS)
    q_scale = jnp.full((1, FOX_WIDTH), FOX_DIM ** -0.5 * LOG2E, F32)
    g_scale = jnp.concatenate([jnp.full((GLA_KEY_WIDTH,), GLA_DK ** -0.5, F32),
                               jnp.ones((G_WIDTH - GLA_KEY_WIDTH,), F32)]).reshape(1, G_WIDTH)
    zero_state = jnp.zeros((bp, GLA_HEADS, GLA_DK, GLA_DV), F32)
    w = _prep_weights(w_in, b_f, w_a2, b_a, w_out, w_gate, w_up, w_down)
    conv_b3 = conv_b.reshape(DEPTH, 1, D_FF)
    norm_attn = norm_attn.reshape(DEPTH, 1, D_MODEL)
    norm_ffn = norm_ffn.reshape(DEPTH, 1, D_MODEL)
    fox_norm = fox_norm.reshape(DEPTH, 1, FOX_WIDTH)
    gla_norm = gla_norm.reshape(DEPTH, 1, GLA_WIDTH)
    small = (w["w_small"], w["b_small"], w["wa2_pad"], w["b_a"], w["wf_t"], w["bf_t"])

    def projections(xn, l, tm, k_stack, v_stack):
        wide = 2 * TN
        (q,) = matmul([xn], [w["w_q"]], layer=l, out_dtypes=[BF16], tm=tm, tn=wide, n=FOX_WIDTH, scale=q_scale)
        k_stack, k16 = matmul([xn], [w["w_k"]], layer=l, out_dtypes=[F32, BF16], tm=tm, tn=wide, n=FOX_WIDTH,
                              stacked=(k_stack,))
        v_stack, v16 = matmul([xn], [w["w_v"]], layer=l, out_dtypes=[F32, BF16], tm=tm, tn=wide, n=FOX_WIDTH,
                              stacked=(v_stack,))
        (gp,) = matmul([xn], [w["w_g"]], layer=l, out_dtypes=[BF16], tm=tm, tn=wide, n=G_WIDTH, scale=g_scale)
        return q, k_stack, k16, v_stack, v16, gp

    def out_and_down(fo, go, h_fn, x, l, tm_out, tm):
        (x,) = matmul([fo, go], [w["w_out"], w["w_out"]], layer=l, out_dtypes=[F32], tm=tm_out, tn=D_MODEL,
                      n=D_MODEL, w_rows=[0, 1], res=x)
        h, extra = h_fn(x)
        (x,) = matmul([h], [w["w_down"]], layer=l, out_dtypes=[F32], tm=tm, tn=TN, n=D_MODEL, res=x)
        return x, extra

    outs = {k: [] for k in ("lp", "sp", "cp", "ls", "ss", "cs")}
    kp = vp = ksm = vsm = None
    for l in range(DEPTH):
        xn, logf, la, _, cx = norm_small(xp, norm_attn, *small, layer=l, tm=TM, seg=seq, emit_cx=True)
        q, kp, k16, vp, v16, gp = projections(xn, l, TM, kp, vp)
        fo = fox_prompt(q, k16, cx, v16, fox_norm, layer=l, batch=bp, seq=seq)
        go, s_fin = gla(gp, la, gla_norm, zero_state, layer=l, batch=bp, seq=seq)
        ffn = lambda x: ffn_up_prompt(x, norm_ffn, w["w_gate"], w["w_up"], conv_w, conv_b3,
                                      layer=l, seq=seq, tm=TM, tn=TN)
        xp, gtail = out_and_down(fo, go, ffn, xp, l, TM // 2, TM)
        outs["lp"].append(logf.reshape(bp, seq, FOX_HEADS))
        outs["sp"].append(s_fin)
        outs["cp"].append(gtail.reshape(bp, seq // TM, SUBLANES, D_FF)[:, -1, SUBLANES - (CONV_W - 1):, :])

        xn, logf, la, c = norm_small(xs, norm_attn, *small, layer=l, tm=ms, seg=t_new, emit_cx=False)
        q, ksm, k16, vsm, v16, gp = projections(xn, l, ms, ksm, vsm)
        kn_page = _pad_rows(k16, db, t_new, PAGE).reshape(db * PAGE * FOX_HEADS, FOX_DIM)
        vn_page = _pad_rows(v16, db, t_new, PAGE).reshape(db * PAGE * FOX_HEADS, FOX_DIM)
        c_new = c.reshape(FOX_HEADS, db, t_new).transpose(1, 2, 0)
        c_new_flat = jnp.pad(c_new, ((0, 0), (0, PAGE - t_new), (0, 0))).reshape(db, 1, PAGE * FOX_HEADS)
        fo = fox_sample(page_table, q.astype(F32), kn_page, vn_page, c_new_flat, fox_norm,
                        cache_k4, cache_v4, cache_lf_flat, layer=l, pages=SAMPLE_PAGES)
        gp_pad = _pad_rows(gp, db, t_new, GLA_CHUNK)
        la_pad = _pad_rows(la, db, t_new, GLA_CHUNK)
        go_pad, s_fin = gla(gp_pad, la_pad, gla_norm, state_gla[l], layer=l, batch=db, seq=GLA_CHUNK)
        go = go_pad.reshape(db, GLA_CHUNK, GLA_WIDTH)[:, :t_new].reshape(ms, GLA_WIDTH)
        st = state_conv[l]
        zeros_f = jnp.zeros((db, t_new - 1, D_FF), F32)
        e1 = jnp.concatenate([st[:, 1:2], zeros_f], axis=1).reshape(ms, D_FF)
        e2 = jnp.concatenate([st[:, 0:2], zeros_f[:, 1:]], axis=1).reshape(ms, D_FF)
        ffn = lambda x: ffn_up_sample(x, norm_ffn, w["w_gate"], w["w_up"], conv_w, conv_b3, e1, e2,
                                      layer=l, seq=t_new, tn=TN)
        xs, g_all = out_and_down(fo.astype(BF16), go, ffn, xs, l, ms, ms)
        outs["ls"].append(logf.reshape(db, t_new, FOX_HEADS))
        outs["ss"].append(s_fin)
        outs["cs"].append(g_all.reshape(db, t_new, D_FF)[:, t_new - (CONV_W - 1):, :])

    g_fin = norm_final.reshape(1, D_MODEL)
    y_prompt = rmsnorm(xp, g_fin, tm=TM).reshape(bp, seq, D_MODEL)
    y_sample = rmsnorm(xs, g_fin, tm=ms).reshape(db, t_new, D_MODEL)
    stack = lambda key: jnp.stack(outs[key])
    kv_p = lambda t: t.reshape(DEPTH, bp, seq, FOX_HEADS, FOX_DIM)
    kv_s = lambda t: t.reshape(DEPTH, db, t_new, FOX_HEADS, FOX_DIM)
    return (y_prompt, y_sample,
            kv_p(kp), kv_p(vp), stack("lp"), stack("sp"), stack("cp"),
            kv_s(ksm), kv_s(vsm), stack("ls"), stack("ss"), stack("cs"))
```

```python
import functools
import math

import jax
import jax.numpy as jnp
from jax import lax
from jax.experimental import pallas as pl
from jax.experimental.pallas import tpu as pltpu

F32 = jnp.float32
BF16 = jnp.bfloat16

D_MODEL = 2048
DEPTH = 4
PAGE = 128
FOX_HEADS = 8
FOX_DIM = 128
FOX_WIDTH = FOX_HEADS * FOX_DIM
GLA_HEADS = 4
GLA_DK = 128
GLA_DV = 256
GLA_KEY_WIDTH = GLA_HEADS * GLA_DK
GLA_WIDTH = GLA_HEADS * GLA_DV
GLA_RANK = 16
GLA_NORMALIZER = 16.0
D_FF = 5632
CONV_W = 3
EPS = 1e-6
LOG2E = math.log2(math.e)

LANES = 128
SUBLANES = 8
BF16_ROWS = 16
MXU_COLS = 256
VMEM_LIMIT = 56 * 1024 * 1024

GLA_CHUNK = 128
GLA_SUB = 8
GLA_HEAD_BLOCK = 4
NEG_BIG = -1e30

TM = 1024
TN = 512
TQ = 512
SAMPLE_PAGES = 16

G_GQ, G_GK, G_GV, G_GR = 0, 512, 1024, 2048
G_WIDTH = 3072


def _params(sem):
    return pltpu.CompilerParams(dimension_semantics=sem, vmem_limit_bytes=VMEM_LIMIT)


def _log_sigmoid(x):
    return jnp.minimum(x, 0.0) - jnp.log(1.0 + jnp.exp(-jnp.abs(x)))


def _silu(x):
    return x * (1.0 / (1.0 + jnp.exp(-x)))


def _rms(x, g):
    return x * lax.rsqrt(jnp.mean(x * x, axis=-1, keepdims=True) + EPS) * g


def _dot(a, b):
    return jnp.dot(a, b, preferred_element_type=F32)


def _dot_nt(a, b):
    return lax.dot_general(a, b, (((1,), (1,)), ((), ())), preferred_element_type=F32)


def _dot_tn(a, b):
    return lax.dot_general(a, b, (((0,), (0,)), ((), ())), preferred_element_type=F32)


def _split3(x):
    hi = x.astype(BF16).astype(F32)
    r1 = x - hi
    mid = r1.astype(BF16).astype(F32)
    lo = (r1 - mid).astype(BF16).astype(F32)
    return hi, mid, lo


def _norm_small_kernel(x_ref, g_ref, ws_ref, bs_ref, wa2_ref, ba_ref, wft_ref, bft_ref,
                       xn_ref, logf_ref, la_ref, c_ref, *rest, tm, seg, emit_cx):
    if emit_cx:
        cx_ref, carry_ref = rest
    else:
        (carry_ref,) = rest
    i = pl.program_id(0)
    xn = _rms(x_ref[...], g_ref[0]).astype(BF16)
    xn_ref[...] = xn
    zs = _dot(xn, ws_ref[0])
    logf_ref[...] = _log_sigmoid(zs + bs_ref[0])[:, :FOX_HEADS]
    la_pre = _dot(zs.astype(BF16), wa2_ref[0]) + ba_ref[0]
    la_ref[...] = _log_sigmoid(la_pre) * (1.0 / GLA_NORMALIZER)

    lft = _log_sigmoid(_dot_nt(wft_ref[0], xn) + bft_ref[0])[:FOX_HEADS]
    lane = lax.broadcasted_iota(jnp.int32, (FOX_HEADS, tm), 1)
    pos = lane & (seg - 1) if seg < tm else lane
    c = lft
    d = 1
    while d < min(seg, tm):
        c = c + jnp.where(pos >= d, pltpu.roll(c, d, axis=1), 0.0)
        d *= 2
    if seg > tm:
        tiles_per_seq = seg // tm

        @pl.when(i % tiles_per_seq == 0)
        def _():
            carry_ref[...] = jnp.zeros_like(carry_ref)

        c = c + carry_ref[:, 0:1]
        carry_ref[...] = jnp.broadcast_to(c[:, tm - 1:tm], carry_ref.shape)
    c_ref[0] = c

    if emit_cx:
        c2 = c * (-LOG2E)
        stacked = jnp.concatenate(
            [c2, c2, c2, jnp.zeros((LANES - 3 * FOX_HEADS, tm), F32)], axis=0)
        hi, mid, lo = _split3(stacked)
        rowid = lax.broadcasted_iota(jnp.int32, stacked.shape, 0)
        pieces = jnp.where(rowid < FOX_HEADS, hi, jnp.where(rowid < 2 * FOX_HEADS, mid, lo))
        cx_ref[...] = pieces.T.astype(BF16)


def norm_small(x, gamma, w_small, b_small, wa2_pad, b_a, wf_t, bf_t, *, layer, tm, seg, emit_cx):
    m = x.shape[0]
    n_tiles = m // tm
    if seg >= tm:
        tps = seg // tm
        c_shape = (m // seg, FOX_HEADS, seg)
        c_map = lambda i: (i // tps, 0, i % tps)
    else:
        c_shape = (n_tiles, FOX_HEADS, tm)
        c_map = lambda i: (i, 0, 0)
    lay = lambda i: (layer, 0, 0)
    out_specs = [
        pl.BlockSpec((tm, D_MODEL), lambda i: (i, 0)),
        pl.BlockSpec((tm, FOX_HEADS), lambda i: (i, 0)),
        pl.BlockSpec((tm, GLA_KEY_WIDTH), lambda i: (i, 0)),
        pl.BlockSpec((1, FOX_HEADS, tm), c_map),
    ]
    out_shape = [
        jax.ShapeDtypeStruct((m, D_MODEL), BF16),
        jax.ShapeDtypeStruct((m, FOX_HEADS), F32),
        jax.ShapeDtypeStruct((m, GLA_KEY_WIDTH), F32),
        jax.ShapeDtypeStruct(c_shape, F32),
    ]
    if emit_cx:
        out_specs.append(pl.BlockSpec((tm, LANES), lambda i: (i, 0)))
        out_shape.append(jax.ShapeDtypeStruct((m, LANES), BF16))
    return pl.pallas_call(
        functools.partial(_norm_small_kernel, tm=tm, seg=seg, emit_cx=emit_cx),
        grid=(n_tiles,),
        in_specs=[
            pl.BlockSpec((tm, D_MODEL), lambda i: (i, 0)),
            pl.BlockSpec((1, 1, D_MODEL), lay),
            pl.BlockSpec((1, D_MODEL, LANES), lay),
            pl.BlockSpec((1, 1, LANES), lay),
            pl.BlockSpec((1, LANES, GLA_KEY_WIDTH), lay),
            pl.BlockSpec((1, 1, GLA_KEY_WIDTH), lay),
            pl.BlockSpec((1, BF16_ROWS, D_MODEL), lay),
            pl.BlockSpec((1, BF16_ROWS, 1), lay),
        ],
        out_specs=out_specs,
        out_shape=out_shape,
        scratch_shapes=[pltpu.VMEM((FOX_HEADS, LANES), F32)],
        compiler_params=_params(("arbitrary",)),
        name="norm_small",
    )(x, gamma, w_small, b_small, wa2_pad, b_a, wf_t, bf_t)


def _mm_kernel(*refs, n_a, has_scale, has_res, has_stack_in, n_out):
    a_refs = refs[:n_a]
    w_refs = refs[n_a:2 * n_a]
    pos = 2 * n_a
    acc = _dot(a_refs[0][...], w_refs[0][0].astype(BF16))
    for a_ref, w_ref in zip(a_refs[1:], w_refs[1:]):
        acc = acc + _dot(a_ref[...], w_ref[0].astype(BF16))
    if has_scale:
        acc = acc * refs[pos][...]
        pos += 1
    if has_res:
        acc = refs[pos][...] + acc
        pos += 1
    if has_stack_in:
        pos += 1
    for o_ref in refs[pos:pos + n_out]:
        o_ref[...] = acc.astype(o_ref.dtype)


def matmul(a_list, w_list, *, layer, out_dtypes, tm, tn, n, w_rows=None, w_col0=0, scale=None,
           res=None, stacked=None):
    m = a_list[0].shape[0]
    n_i = m // tm
    w_rows = w_rows or [0] * len(a_list)
    in_specs = [pl.BlockSpec((tm, a.shape[1]), lambda i, j: (i, 0)) for a in a_list]
    for a, w, r in zip(a_list, w_list, w_rows):
        in_specs.append(pl.BlockSpec((1, a.shape[1], tn), lambda i, j, r=r: (layer, r, w_col0 + j)))
    args = list(a_list) + list(w_list)
    if scale is not None:
        in_specs.append(pl.BlockSpec((1, tn), lambda i, j: (0, j)))
        args.append(scale)
    if res is not None:
        in_specs.append(pl.BlockSpec((tm, tn), lambda i, j: (i, j)))
        args.append(res)
    out_specs = [pl.BlockSpec((tm, tn), lambda i, j: (i, j)) for _ in out_dtypes]
    out_shape = [jax.ShapeDtypeStruct((m, n), dt) for dt in out_dtypes]
    aliases = {}
    has_stack_in = False
    if stacked is not None:
        out_specs[0] = pl.BlockSpec((tm, tn), lambda i, j: (layer * n_i + i, j))
        out_shape[0] = jax.ShapeDtypeStruct((DEPTH * m, n), out_dtypes[0])
        if stacked[0] is not None:
            has_stack_in = True
            in_specs.append(pl.BlockSpec(memory_space=pl.ANY))
            aliases = {len(args): 0}
            args.append(stacked[0])
    outs = pl.pallas_call(
        functools.partial(_mm_kernel, n_a=len(a_list), has_scale=scale is not None,
                          has_res=res is not None, has_stack_in=has_stack_in, n_out=len(out_dtypes)),
        grid=(n_i, n // tn),
        in_specs=in_specs,
        out_specs=out_specs,
        out_shape=out_shape,
        input_output_aliases=aliases,
        compiler_params=_params(("parallel", "arbitrary")),
        name="matmul",
    )(*args)
    return outs


def _fox_prompt_kernel(q_ref, k_ref, cx_ref, v_ref, gn_ref, o_ref, kx_sc, vt_sc, *, seq, tq):
    h = pl.program_id(1)
    kx_sc[:, 0:FOX_DIM] = k_ref[...]
    kx_sc[:, FOX_DIM:2 * FOX_DIM] = cx_ref[...]
    vt_sc[0:FOX_DIM, :] = v_ref[...].astype(F32).T.astype(BF16)
    vt_sc[FOX_DIM:FOX_DIM + BF16_ROWS, :] = jnp.ones((BF16_ROWS, seq), BF16)

    lane = lax.broadcasted_iota(jnp.int32, (tq, FOX_DIM), 1)
    pick = (lane == h) | (lane == h + FOX_HEADS) | (lane == h + 2 * FOX_HEADS)
    onehot = jnp.where(pick, 1.0, 0.0).astype(BF16)
    key_i = lax.broadcasted_iota(jnp.int32, (tq, tq), 0)
    qry_i = lax.broadcasted_iota(jnp.int32, (tq, tq), 1)
    causal = key_i <= qry_i

    def scores(qi):
        lo, hi = qi * tq, (qi + 1) * tq
        qx = jnp.concatenate([q_ref[lo:hi, :], onehot], axis=1)
        s_diag = jnp.where(causal, _dot_nt(kx_sc[lo:hi, :], qx), NEG_BIG)
        s_off = _dot_nt(kx_sc[0:lo, :], qx) if qi > 0 else None
        return s_diag, s_off

    n_q = seq // tq
    ahead = scores(0)
    for qi in range(n_q):
        lo, hi = qi * tq, (qi + 1) * tq
        s_diag, s_off = ahead
        if qi + 1 < n_q:
            ahead = scores(qi + 1)
        m = jnp.max(s_diag, axis=0, keepdims=True)
        if qi > 0:
            m = jnp.maximum(m, jnp.max(s_off, axis=0, keepdims=True))
            p_off = jnp.exp2(s_off - m).astype(BF16)
            ot = _dot(vt_sc[:, 0:lo], p_off)
        p_diag = jnp.exp2(s_diag - m).astype(BF16)
        ot_d = _dot(vt_sc[:, lo:hi], p_diag)
        ot = ot + ot_d if qi > 0 else ot_d
        o = ot[0:FOX_DIM, :] * (1.0 / ot[FOX_DIM:FOX_DIM + 1, :])
        y = o * lax.rsqrt(jnp.mean(o * o, axis=0, keepdims=True) + EPS)
        o_ref[lo:hi, :] = (y.T * gn_ref[0]).astype(o_ref.dtype)


def fox_prompt(q, k16, cx, v16, fox_norm, *, layer, batch, seq):
    m = batch * seq
    return pl.pallas_call(
        functools.partial(_fox_prompt_kernel, seq=seq, tq=TQ),
        grid=(batch, FOX_HEADS),
        in_specs=[
            pl.BlockSpec((seq, FOX_DIM), lambda b, h: (b, h)),
            pl.BlockSpec((seq, FOX_DIM), lambda b, h: (b, h)),
            pl.BlockSpec((seq, LANES), lambda b, h: (b, 0)),
            pl.BlockSpec((seq, FOX_DIM), lambda b, h: (b, h)),
            pl.BlockSpec((1, 1, FOX_DIM), lambda b, h: (layer, 0, h)),
        ],
        out_specs=pl.BlockSpec((seq, FOX_DIM), lambda b, h: (b, h)),
        out_shape=jax.ShapeDtypeStruct((m, FOX_WIDTH), BF16),
        scratch_shapes=[pltpu.VMEM((seq, 2 * FOX_DIM), BF16),
                        pltpu.VMEM((FOX_DIM + BF16_ROWS, seq), BF16)],
        compiler_params=_params(("parallel", "arbitrary")),
        name="fox_prompt",
    )(q, k16, cx, v16, fox_norm)


def _fox_sample_kernel(pt_ref, q_ref, kn_ref, vn_ref, cn_ref, gn_ref, *rest, pages, t_new):
    k_refs = rest[:pages]
    v_refs = rest[pages:2 * pages]
    lf_refs = rest[2 * pages:3 * pages]
    o_ref, m_sc, l_sc, acc_sc, carry_sc = rest[3 * pages:]
    del pt_ref
    step = pl.program_id(1)
    n_steps = pl.num_programs(1)
    rows = t_new

    @pl.when(step == 0)
    def _():
        m_sc[...] = jnp.full_like(m_sc, NEG_BIG)
        l_sc[...] = jnp.zeros_like(l_sc)
        acc_sc[...] = jnp.zeros_like(acc_sc)
        carry_sc[...] = jnp.zeros_like(carry_sc)

    flat = PAGE * FOX_HEADS
    lane = lax.broadcasted_iota(jnp.int32, (pages, flat), 1)
    lf = jnp.concatenate([lf_refs[p][0, 0] for p in range(pages)], axis=0)
    suf, tot = lf, lf
    d = FOX_HEADS
    while d < flat:
        suf = suf + jnp.where(lane < flat - d, pltpu.roll(suf, flat - d, axis=1), 0.0)
        tot = tot + pltpu.roll(tot, d, axis=1)
        d *= 2
    run = carry_sc[0:1, :]
    bias = [None] * pages
    for p in range(pages - 1, -1, -1):
        bias[p] = ((suf[p:p + 1, :] - lf[p:p + 1, :]) + run) * LOG2E
        run = run + tot[p:p + 1, :]
    carry_sc[0:1, :] = run

    n_rows = FOX_HEADS * rows
    q_flat = jnp.concatenate(
        [q_ref[:, h * FOX_DIM:(h + 1) * FOX_DIM] for h in range(FOX_HEADS)], axis=0).astype(BF16)
    zeros = jnp.zeros_like(q_flat)
    q_pair = jnp.concatenate([jnp.concatenate([q_flat, zeros], axis=1),
                              jnp.concatenate([zeros, q_flat], axis=1)], axis=0)
    row_i = lax.broadcasted_iota(jnp.int32, (n_rows, flat), 0)
    col_i = lax.broadcasted_iota(jnp.int32, (n_rows, flat), 1)
    own = (row_i >> (rows.bit_length() - 1)) == (col_i & (FOX_HEADS - 1))

    def local_softmax(s, v_all):
        m = jnp.max(s, axis=-1, keepdims=True)
        pr = jnp.exp2(s - m)
        return m, jnp.sum(pr, axis=-1, keepdims=True), _dot(pr.astype(BF16), v_all)

    def merge(groups):
        m_old = m_sc[...]
        m_new = m_old
        for m, _, _ in groups:
            m_new = jnp.maximum(m_new, m)
        alpha = jnp.exp2(m_old - m_new)
        l_new = alpha * l_sc[...]
        acc = alpha * acc_sc[...]
        for m, l, o in groups:
            w = jnp.exp2(m - m_new)
            l_new = l_new + w * l
            acc = acc + w * o
        m_sc[...] = m_new
        l_sc[...] = l_new
        acc_sc[...] = acc

    s_parts = [None] * pages
    for a in range(0, pages, 2):
        kk = jnp.concatenate([k_refs[a][0, 0], k_refs[a + 1][0, 0]], axis=1).astype(BF16)
        s2 = _dot_nt(q_pair, kk)
        s_parts[a] = jnp.where(own, s2[0:n_rows] + bias[a], NEG_BIG)
        s_parts[a + 1] = jnp.where(own, s2[n_rows:2 * n_rows] + bias[a + 1], NEG_BIG)
    half = pages // 2
    groups = []
    for lo_p, hi_p in ((0, half), (half, pages)):
        groups.append(local_softmax(
            jnp.concatenate(s_parts[lo_p:hi_p], axis=1),
            jnp.concatenate([v_refs[p][0, 0].astype(BF16) for p in range(lo_p, hi_p)], axis=0)))
    merge(groups)

    @pl.when(step == n_steps - 1)
    def _():
        valid = own & ((col_i >> (FOX_HEADS.bit_length() - 1)) <= (row_i & (rows - 1)))
        s_new = _dot_nt(q_flat, kn_ref[...]) - cn_ref[0] * LOG2E
        merge([local_softmax(jnp.where(valid, s_new, NEG_BIG), vn_ref[...])])
        for h in range(FOX_HEADS):
            cols = slice(h * FOX_DIM, (h + 1) * FOX_DIM)
            sl = slice(h * rows, (h + 1) * rows)
            o = acc_sc[sl, :] / l_sc[sl, :]
            o_ref[:, cols] = _rms(o, gn_ref[0, :, cols])


def fox_sample(page_table, q, k_new_page, v_new_page, c_new_flat, fox_norm, cache_k4, cache_v4,
               cache_lf_flat, *, layer, pages):
    db, n_pages = page_table.shape
    t_new = q.shape[0] // db
    assert t_new == SUBLANES and pages % 2 == 0 and n_pages % pages == 0
    n_steps = n_pages // pages
    flat = PAGE * FOX_HEADS
    pt_flat = page_table.reshape(-1)

    def page_map(p):
        def index(b, s, pt):
            return (layer, pt[b * n_pages + (n_steps - 1 - s) * pages + p], 0, 0)
        return index

    in_specs = [
        pl.BlockSpec((t_new, FOX_WIDTH), lambda b, s, pt: (b, 0)),
        pl.BlockSpec((flat, FOX_DIM), lambda b, s, pt: (b, 0)),
        pl.BlockSpec((flat, FOX_DIM), lambda b, s, pt: (b, 0)),
        pl.BlockSpec((1, 1, flat), lambda b, s, pt: (b, 0, 0)),
        pl.BlockSpec((1, 1, FOX_WIDTH), lambda b, s, pt: (layer, 0, 0)),
    ]
    in_specs += [pl.BlockSpec((1, 1, flat, FOX_DIM), page_map(p)) for p in range(pages)]
    in_specs += [pl.BlockSpec((1, 1, flat, FOX_DIM), page_map(p)) for p in range(pages)]
    in_specs += [pl.BlockSpec((1, 1, 1, flat), page_map(p)) for p in range(pages)]
    grid_spec = pltpu.PrefetchScalarGridSpec(
        num_scalar_prefetch=1,
        grid=(db, n_steps),
        in_specs=in_specs,
        out_specs=pl.BlockSpec((t_new, FOX_WIDTH), lambda b, s, pt: (b, 0)),
        scratch_shapes=[pltpu.VMEM((FOX_HEADS * t_new, 1), F32),
                        pltpu.VMEM((FOX_HEADS * t_new, 1), F32),
                        pltpu.VMEM((FOX_HEADS * t_new, FOX_DIM), F32),
                        pltpu.VMEM((SUBLANES, flat), F32)],
    )
    return pl.pallas_call(
        functools.partial(_fox_sample_kernel, pages=pages, t_new=t_new),
        grid_spec=grid_spec,
        out_shape=jax.ShapeDtypeStruct((db * t_new, FOX_WIDTH), F32),
        compiler_params=_params(("parallel", "arbitrary")),
        name="fox_sample",
    )(pt_flat, q, k_new_page, v_new_page, c_new_flat, fox_norm,
      *([cache_k4] * pages), *([cache_v4] * pages), *([cache_lf_flat] * pages))


def _gla_kernel(q_ref, k_ref, v_ref, la_ref, r_ref, gn_ref, s0_ref, o_ref, sfin_ref,
                s_sc, b_sc, *, chunk, sub, hb):
    ci = pl.program_id(2)
    n_chunks = pl.num_programs(2)
    heads = range(hb)
    kcols = lambda h: slice(h * GLA_DK, (h + 1) * GLA_DK)
    vcols = lambda h: slice(h * GLA_DV, (h + 1) * GLA_DV)

    @pl.when(ci == 0)
    def _():
        s_sc[...] = s0_ref[0]

    q = [q_ref[:, kcols(h)].astype(F32) for h in heads]
    k = [k_ref[:, kcols(h)].astype(F32) for h in heads]
    v = [v_ref[:, vcols(h)] for h in heads]
    rowi = lax.broadcasted_iota(jnp.int32, (chunk, chunk), 0)
    coli = lax.broadcasted_iota(jnp.int32, (chunk, chunk), 1)
    row = lax.broadcasted_iota(jnp.int32, (chunk, 1), 0)

    tri = jnp.where(rowi >= coli, 1.0, 0.0).astype(BF16)
    hi, mid, lo = _split3(la_ref[...])
    b_all = _dot(tri, hi.astype(BF16)) + _dot(tri, mid.astype(BF16)) + _dot(tri, lo.astype(BF16))
    b_sc[...] = b_all
    b = [b_all[:, kcols(h)] for h in heads]
    b_last = [b_sc[chunk - 1:chunk, kcols(h)] for h in heads]

    ones = jnp.ones((GLA_DK, chunk), BF16)
    diff = rowi - coli
    sub_shift = sub.bit_length() - 1
    a = [jnp.zeros((chunk, chunk), F32) for _ in heads]
    for d in range(sub):
        for h in heads:
            k_s = k[h] if d == 0 else pltpu.roll(k[h], d, axis=0)
            b_s = b[h] if d == 0 else pltpu.roll(b[h], d, axis=0)
            prod = q[h] * k_s * jnp.exp(b[h] - b_s)
            band = _dot(prod.astype(BF16), ones)
            a[h] = jnp.where(diff == d, band, a[h])
    in_sub = (rowi >> sub_shift) == (coli >> sub_shift)
    a = [jnp.where(in_sub, a[h], 0.0) for h in heads]

    m = sub
    while m < chunk:
        groups = chunk // (2 * m)
        odd = (row & m) != 0
        shift = (2 * m).bit_length() - 1
        same = (rowi >> shift) == (coli >> shift)
        for h in heads:
            ref_rows = [jnp.broadcast_to(b_sc[g * 2 * m + m - 1:g * 2 * m + m, kcols(h)], (2 * m, GLA_DK))
                        for g in range(groups)]
            ref_b = ref_rows[0] if groups == 1 else jnp.concatenate(ref_rows, axis=0)
            e = jnp.exp(-jnp.abs(b[h] - ref_b))
            qm = jnp.where(odd, q[h] * e, 0.0).astype(BF16)
            km = jnp.where(odd, 0.0, k[h] * e).astype(BF16)
            a[h] = a[h] + jnp.where(same, _dot_nt(qm, km), 0.0)
        m *= 2

    s_old = [s_sc[h] for h in heads]
    o = [_dot((q[h] * jnp.exp(b[h])).astype(BF16), s_old[h].astype(BF16))
         + _dot(a[h].astype(BF16), v[h]) for h in heads]

    s_new = []
    for h in heads:
        kl = (k[h] * jnp.exp(b_last[h] - b[h])).astype(BF16)
        decay = jnp.broadcast_to(jnp.exp(b_last[h]), (GLA_DK, GLA_DK)).T
        decay = jnp.concatenate([decay, decay], axis=1)
        s_new.append(decay * s_old[h] + _dot_tn(kl, v[h]))
        s_sc[h] = s_new[h]

    @pl.when(ci == n_chunks - 1)
    def _():
        for h in heads:
            sfin_ref[0, h] = s_new[h]

    for h in heads:
        r = r_ref[:, vcols(h)].astype(F32)
        o_ref[:, vcols(h)] = (_rms(o[h], gn_ref[0, :, vcols(h)]) * _silu(r)).astype(o_ref.dtype)


def gla(gp, la, gla_norm, s0, *, layer, batch, seq):
    m = batch * seq
    chunk = GLA_CHUNK
    hb = GLA_HEAD_BLOCK
    kw, vw = hb * GLA_DK, hb * GLA_DV
    n_chunks = seq // chunk
    rows = lambda b, h, c: b * n_chunks + c
    return pl.pallas_call(
        functools.partial(_gla_kernel, chunk=chunk, sub=GLA_SUB, hb=hb),
        grid=(batch, GLA_HEADS // hb, n_chunks),
        in_specs=[
            pl.BlockSpec((chunk, kw), lambda b, h, c: (rows(b, h, c), G_GQ // kw + h)),
            pl.BlockSpec((chunk, kw), lambda b, h, c: (rows(b, h, c), G_GK // kw + h)),
            pl.BlockSpec((chunk, vw), lambda b, h, c: (rows(b, h, c), G_GV // vw + h)),
            pl.BlockSpec((chunk, kw), lambda b, h, c: (rows(b, h, c), h)),
            pl.BlockSpec((chunk, vw), lambda b, h, c: (rows(b, h, c), G_GR // vw + h)),
            pl.BlockSpec((1, 1, vw), lambda b, h, c: (layer, 0, h)),
            pl.BlockSpec((1, hb, GLA_DK, GLA_DV), lambda b, h, c: (b, h, 0, 0)),
        ],
        out_specs=[
            pl.BlockSpec((chunk, vw), lambda b, h, c: (rows(b, h, c), h)),
            pl.BlockSpec((1, hb, GLA_DK, GLA_DV), lambda b, h, c: (b, h, 0, 0)),
        ],
        out_shape=[
            jax.ShapeDtypeStruct((m, GLA_WIDTH), BF16),
            jax.ShapeDtypeStruct((batch, GLA_HEADS, GLA_DK, GLA_DV), F32),
        ],
        scratch_shapes=[pltpu.VMEM((hb, GLA_DK, GLA_DV), F32), pltpu.VMEM((chunk, kw), F32)],
        compiler_params=_params(("parallel", "parallel", "arbitrary")),
        name="gla",
    )(gp, gp, gp, la, gp, gla_norm, s0)


def _conv_gate(g, g1, g2, u, cw_ref, cb_ref, cols):
    conv = (cb_ref[0, :, cols] + cw_ref[0, 0:1, cols] * g2 + cw_ref[0, 1:2, cols] * g1
            + cw_ref[0, 2:3, cols] * g)
    return _silu(conv) * u


def _ffn_up_prompt_kernel(x_ref, halo_ref, g_ref, wg_ref, wu_ref, cw_ref, cb_ref,
                          h_ref, gtail_ref, xn_sc, *, tm, tn, tiles_per_seq):
    i = pl.program_id(0)
    j = pl.program_id(1)

    @pl.when(j == 0)
    def _():
        xn_sc[0:tm, :] = _rms(x_ref[...], g_ref[0]).astype(BF16)
        keep = jnp.where(i % tiles_per_seq == 0, 0.0, 1.0)
        xn_sc[tm:tm + BF16_ROWS, :] = (_rms(halo_ref[...], g_ref[0]) * keep).astype(BF16)

    for t in range(tn // MXU_COLS):
        cols = slice(t * MXU_COLS, (t + 1) * MXU_COLS)
        g_ext = _dot(xn_sc[...], wg_ref[0, :, cols].astype(BF16))
        u = _dot(xn_sc[0:tm, :], wu_ref[0, :, cols].astype(BF16))
        g = g_ext[0:tm]
        g1 = pltpu.roll(g_ext, 1, axis=0)[0:tm]
        g2 = pltpu.roll(g_ext, 2, axis=0)[0:tm]
        h_ref[:, cols] = _conv_gate(g, g1, g2, u, cw_ref, cb_ref, cols).astype(h_ref.dtype)
        gtail_ref[:, cols] = g[tm - SUBLANES:tm]


def ffn_up_prompt(x, gamma, wg, wu, conv_w, conv_b, *, layer, seq, tm, tn):
    m = x.shape[0]
    tiles_per_seq = seq // tm
    halo_blocks = tm // BF16_ROWS
    return pl.pallas_call(
        functools.partial(_ffn_up_prompt_kernel, tm=tm, tn=tn, tiles_per_seq=tiles_per_seq),
        grid=(m // tm, D_FF // tn),
        in_specs=[
            pl.BlockSpec((tm, D_MODEL), lambda i, j: (i, 0)),
            pl.BlockSpec((BF16_ROWS, D_MODEL), lambda i, j: (jnp.maximum(i * halo_blocks - 1, 0), 0)),
            pl.BlockSpec((1, 1, D_MODEL), lambda i, j: (layer, 0, 0)),
            pl.BlockSpec((1, D_MODEL, tn), lambda i, j: (layer, 0, j)),
            pl.BlockSpec((1, D_MODEL, tn), lambda i, j: (layer, 0, j)),
            pl.BlockSpec((1, CONV_W, tn), lambda i, j: (layer, 0, j)),
            pl.BlockSpec((1, 1, tn), lambda i, j: (layer, 0, j)),
        ],
        out_specs=[
            pl.BlockSpec((tm, tn), lambda i, j: (i, j)),
            pl.BlockSpec((SUBLANES, tn), lambda i, j: (i, j)),
        ],
        out_shape=[
            jax.ShapeDtypeStruct((m, D_FF), BF16),
            jax.ShapeDtypeStruct((m // tm * SUBLANES, D_FF), F32),
        ],
        scratch_shapes=[pltpu.VMEM((tm + BF16_ROWS, D_MODEL), BF16)],
        compiler_params=_params(("parallel", "arbitrary")),
        name="ffn_up_prompt",
    )(x, x, gamma, wg, wu, conv_w, conv_b)


def _ffn_up_sample_kernel(x_ref, g_ref, wg_ref, wu_ref, cw_ref, cb_ref, e1_ref, e2_ref,
                          h_ref, gout_ref, xn_sc, *, seq, tn):
    j = pl.program_id(0)

    @pl.when(j == 0)
    def _():
        xn_sc[...] = _rms(x_ref[...], g_ref[0]).astype(BF16)

    cols = slice(0, tn)
    g = _dot(xn_sc[...], wg_ref[0].astype(BF16))
    u = _dot(xn_sc[...], wu_ref[0].astype(BF16))
    pos = lax.broadcasted_iota(jnp.int32, (g.shape[0], 1), 0) & (seq - 1)
    g1 = jnp.where(pos >= 1, pltpu.roll(g, 1, axis=0), e1_ref[...])
    g2 = jnp.where(pos >= 2, pltpu.roll(g, 2, axis=0), e2_ref[...])
    h_ref[...] = _conv_gate(g, g1, g2, u, cw_ref, cb_ref, cols).astype(h_ref.dtype)
    gout_ref[...] = g


def ffn_up_sample(x, gamma, wg, wu, conv_w, conv_b, e1, e2, *, layer, seq, tn):
    m = x.shape[0]
    return pl.pallas_call(
        functools.partial(_ffn_up_sample_kernel, seq=seq, tn=tn),
        grid=(D_FF // tn,),
        in_specs=[
            pl.BlockSpec((m, D_MODEL), lambda j: (0, 0)),
            pl.BlockSpec((1, 1, D_MODEL), lambda j: (layer, 0, 0)),
            pl.BlockSpec((1, D_MODEL, tn), lambda j: (layer, 0, j)),
            pl.BlockSpec((1, D_MODEL, tn), lambda j: (layer, 0, j)),
            pl.BlockSpec((1, CONV_W, tn), lambda j: (layer, 0, j)),
            pl.BlockSpec((1, 1, tn), lambda j: (layer, 0, j)),
            pl.BlockSpec((m, tn), lambda j: (0, j)),
            pl.BlockSpec((m, tn), lambda j: (0, j)),
        ],
        out_specs=[
            pl.BlockSpec((m, tn), lambda j: (0, j)),
            pl.BlockSpec((m, tn), lambda j: (0, j)),
        ],
        out_shape=[
            jax.ShapeDtypeStruct((m, D_FF), BF16),
            jax.ShapeDtypeStruct((m, D_FF), F32),
        ],
        scratch_shapes=[pltpu.VMEM((m, D_MODEL), BF16)],
        compiler_params=_params(("arbitrary",)),
        name="ffn_up_sample",
    )(x, gamma, wg, wu, conv_w, conv_b, e1, e2)


def _rmsnorm_kernel(x_ref, g_ref, o_ref):
    o_ref[...] = _rms(x_ref[...], g_ref[...])


def rmsnorm(x, gamma, *, tm):
    m = x.shape[0]
    return pl.pallas_call(
        _rmsnorm_kernel,
        grid=(m // tm,),
        in_specs=[pl.BlockSpec((tm, D_MODEL), lambda i: (i, 0)),
                  pl.BlockSpec((1, D_MODEL), lambda i: (0, 0))],
        out_specs=pl.BlockSpec((tm, D_MODEL), lambda i: (i, 0)),
        out_shape=jax.ShapeDtypeStruct((m, D_MODEL), F32),
        compiler_params=_params(("parallel",)),
        name="rmsnorm",
    )(x, gamma)


def _prep_weights(w_in, b_f, w_a2, b_a, w_out, w_gate, w_up, w_down):
    o_fl = 3 * FOX_WIDTH
    o_gq = o_fl + FOX_HEADS
    o_ga = o_gq + 2 * GLA_KEY_WIDTH + 2 * GLA_WIDTH
    cols = lambda a, n: w_in[:, :, a:a + n]
    w_fl = cols(o_fl, FOX_HEADS)
    w_ga = cols(o_ga, GLA_RANK)
    pad = LANES - FOX_HEADS - GLA_RANK
    zeros = lambda *s: jnp.zeros((DEPTH,) + s, F32)
    return dict(
        w_g=cols(o_gq, G_WIDTH).astype(BF16),
        w_small=jnp.concatenate([w_fl, w_ga, zeros(D_MODEL, pad)], axis=2).astype(BF16),
        b_small=jnp.concatenate([b_f, zeros(LANES - FOX_HEADS)], axis=1).reshape(DEPTH, 1, LANES),
        wa2_pad=jnp.concatenate([zeros(FOX_HEADS, GLA_KEY_WIDTH), w_a2, zeros(pad, GLA_KEY_WIDTH)],
                                axis=1).astype(BF16),
        b_a=b_a.reshape(DEPTH, 1, GLA_KEY_WIDTH),
        wf_t=jnp.pad(jnp.swapaxes(w_fl, 1, 2), ((0, 0), (0, BF16_ROWS - FOX_HEADS), (0, 0))).astype(BF16),
        bf_t=jnp.pad(b_f.reshape(DEPTH, FOX_HEADS, 1), ((0, 0), (0, BF16_ROWS - FOX_HEADS), (0, 0))),
        w_out=w_out.astype(BF16), w_gate=w_gate, w_up=w_up,
        w_down=w_down.astype(BF16),
    )


def _pad_rows(x, batch, t, t_pad):
    w = x.shape[-1]
    return jnp.pad(x.reshape(batch, t, w), ((0, 0), (0, t_pad - t), (0, 0))).reshape(batch * t_pad, w)


def kernel(x_prompt, x_sample, cache_k, cache_v, cache_logf, state_gla, state_conv, page_table,
           w_in, b_f, w_a2, b_a, fox_norm, gla_norm, w_out, norm_attn, norm_ffn,
           w_gate, w_up, conv_w, conv_b, w_down, norm_final):
    bp, seq, _ = x_prompt.shape
    db, t_new, _ = x_sample.shape
    n_pool = cache_k.shape[1]
    mp, ms = bp * seq, db * t_new

    xp = x_prompt.reshape(mp, D_MODEL)
    xs = x_sample.reshape(ms, D_MODEL)
    cache_k4 = cache_k.reshape(DEPTH, n_pool, PAGE * FOX_HEADS, FOX_DIM)
    cache_v4 = cache_v.reshape(DEPTH, n_pool, PAGE * FOX_HEADS, FOX_DIM)
    cache_lf_flat = cache_logf.reshape(DEPTH, n_pool, 1, PAGE * FOX_HEADS)
    q_scale = jnp.full((1, FOX_WIDTH), FOX_DIM ** -0.5 * LOG2E, F32)
    g_scale = jnp.concatenate([jnp.full((GLA_KEY_WIDTH,), GLA_DK ** -0.5, F32),
                               jnp.ones((G_WIDTH - GLA_KEY_WIDTH,), F32)]).reshape(1, G_WIDTH)
    zero_state = jnp.zeros((bp, GLA_HEADS, GLA_DK, GLA_DV), F32)
    w = _prep_weights(w_in, b_f, w_a2, b_a, w_out, w_gate, w_up, w_down)
    conv_b3 = conv_b.reshape(DEPTH, 1, D_FF)
    norm_attn = norm_attn.reshape(DEPTH, 1, D_MODEL)
    norm_ffn = norm_ffn.reshape(DEPTH, 1, D_MODEL)
    fox_norm = fox_norm.reshape(DEPTH, 1, FOX_WIDTH)
    gla_norm = gla_norm.reshape(DEPTH, 1, GLA_WIDTH)
    small = (w["w_small"], w["b_small"], w["wa2_pad"], w["b_a"], w["wf_t"], w["bf_t"])

    def projections(xn, l, tm, k_stack, v_stack):
        wide = 2 * TN
        (q,) = matmul([xn], [w_in], layer=l, out_dtypes=[BF16], tm=tm, tn=wide, n=FOX_WIDTH, w_col0=0,
                      scale=q_scale)
        k_stack, k16 = matmul([xn], [w_in], layer=l, out_dtypes=[F32, BF16], tm=tm, tn=wide, n=FOX_WIDTH,
                              w_col0=1, stacked=(k_stack,))
        v_stack, v16 = matmul([xn], [w_in], layer=l, out_dtypes=[F32, BF16], tm=tm, tn=wide, n=FOX_WIDTH,
                              w_col0=2, stacked=(v_stack,))
        (gp,) = matmul([xn], [w["w_g"]], layer=l, out_dtypes=[BF16], tm=tm, tn=wide, n=G_WIDTH, scale=g_scale)
        return q, k_stack, k16, v_stack, v16, gp

    def out_and_down(fo, go, h_fn, x, l, tm_out, tm):
        (x,) = matmul([fo, go], [w["w_out"], w["w_out"]], layer=l, out_dtypes=[F32], tm=tm_out, tn=D_MODEL,
                      n=D_MODEL, w_rows=[0, 1], res=x)
        h, extra = h_fn(x)
        (x,) = matmul([h], [w["w_down"]], layer=l, out_dtypes=[F32], tm=tm, tn=TN, n=D_MODEL, res=x)
        return x, extra

    outs = {k: [] for k in ("lp", "sp", "cp", "ls", "ss", "cs")}
    kp = vp = ksm = vsm = None
    for l in range(DEPTH):
        xn, logf, la, _, cx = norm_small(xp, norm_attn, *small, layer=l, tm=TM, seg=seq, emit_cx=True)
        q, kp, k16, vp, v16, gp = projections(xn, l, TM, kp, vp)
        fo = fox_prompt(q, k16, cx, v16, fox_norm, layer=l, batch=bp, seq=seq)
        go, s_fin = gla(gp, la, gla_norm, zero_state, layer=l, batch=bp, seq=seq)
        ffn = lambda x: ffn_up_prompt(x, norm_ffn, w["w_gate"], w["w_up"], conv_w, conv_b3,
                                      layer=l, seq=seq, tm=TM, tn=TN)
        xp, gtail = out_and_down(fo, go, ffn, xp, l, TM // 2, TM)
        outs["lp"].append(logf.reshape(bp, seq, FOX_HEADS))
        outs["sp"].append(s_fin)
        outs["cp"].append(gtail.reshape(bp, seq // TM, SUBLANES, D_FF)[:, -1, SUBLANES - (CONV_W - 1):, :])

        xn, logf, la, c = norm_small(xs, norm_attn, *small, layer=l, tm=ms, seg=t_new, emit_cx=False)
        q, ksm, k16, vsm, v16, gp = projections(xn, l, ms, ksm, vsm)
        kn_page = _pad_rows(k16, db, t_new, PAGE).reshape(db * PAGE * FOX_HEADS, FOX_DIM)
        vn_page = _pad_rows(v16, db, t_new, PAGE).reshape(db * PAGE * FOX_HEADS, FOX_DIM)
        c_new = c.reshape(FOX_HEADS, db, t_new).transpose(1, 2, 0)
        c_new_flat = jnp.pad(c_new, ((0, 0), (0, PAGE - t_new), (0, 0))).reshape(db, 1, PAGE * FOX_HEADS)
        fo = fox_sample(page_table, q.astype(F32), kn_page, vn_page, c_new_flat, fox_norm,
                        cache_k4, cache_v4, cache_lf_flat, layer=l, pages=SAMPLE_PAGES)
        gp_pad = _pad_rows(gp, db, t_new, GLA_CHUNK)
        la_pad = _pad_rows(la, db, t_new, GLA_CHUNK)
        go_pad, s_fin = gla(gp_pad, la_pad, gla_norm, state_gla[l], layer=l, batch=db, seq=GLA_CHUNK)
        go = go_pad.reshape(db, GLA_CHUNK, GLA_WIDTH)[:, :t_new].reshape(ms, GLA_WIDTH)
        st = state_conv[l]
        zeros_f = jnp.zeros((db, t_new - 1, D_FF), F32)
        e1 = jnp.concatenate([st[:, 1:2], zeros_f], axis=1).reshape(ms, D_FF)
        e2 = jnp.concatenate([st[:, 0:2], zeros_f[:, 1:]], axis=1).reshape(ms, D_FF)
        ffn = lambda x: ffn_up_sample(x, norm_ffn, w["w_gate"], w["w_up"], conv_w, conv_b3, e1, e2,
                                      layer=l, seq=t_new, tn=TN)
        xs, g_all = out_and_down(fo.astype(BF16), go, ffn, xs, l, ms, ms)
        outs["ls"].append(logf.reshape(db, t_new, FOX_HEADS))
        outs["ss"].append(s_fin)
        outs["cs"].append(g_all.reshape(db, t_new, D_FF)[:, t_new - (CONV_W - 1):, :])

    g_fin = norm_final.reshape(1, D_MODEL)
    y_prompt = rmsnorm(xp, g_fin, tm=TM).reshape(bp, seq, D_MODEL)
    y_sample = rmsnorm(xs, g_fin, tm=ms).reshape(db, t_new, D_MODEL)
    stack = lambda key: jnp.stack(outs[key])
    kv_p = lambda t: t.reshape(DEPTH, bp, seq, FOX_HEADS, FOX_DIM)
    kv_s = lambda t: t.reshape(DEPTH, db, t_new, FOX_HEADS, FOX_DIM)
    return (y_prompt, y_sample,
            kv_p(kp), kv_p(vp), stack("lp"), stack("sp"), stack("cp"),
            kv_s(ksm), kv_s(vsm), stack("ls"), stack("ss"), stack("cs"))
```

```python
import functools
import math

import jax
import jax.numpy as jnp
from jax import lax
from jax.experimental import pallas as pl
from jax.experimental.pallas import tpu as pltpu

F32 = jnp.float32
BF16 = jnp.bfloat16

D_MODEL = 2048
DEPTH = 4
PAGE = 128
FOX_HEADS = 8
FOX_DIM = 128
FOX_WIDTH = FOX_HEADS * FOX_DIM
GLA_HEADS = 4
GLA_DK = 128
GLA_DV = 256
GLA_KEY_WIDTH = GLA_HEADS * GLA_DK
GLA_WIDTH = GLA_HEADS * GLA_DV
GLA_RANK = 16
GLA_NORMALIZER = 16.0
D_FF = 5632
CONV_W = 3
EPS = 1e-6
LOG2E = math.log2(math.e)

LANES = 128
SUBLANES = 8
BF16_ROWS = 16
MXU_COLS = 256
VMEM_LIMIT = 56 * 1024 * 1024

GLA_CHUNK = 128
GLA_SUB = 8
GLA_HEAD_BLOCK = 4
NEG_BIG = -1e30

TM = 1024
TN = 512
TQ = 512
SAMPLE_PAGES = 16

G_GQ, G_GK, G_GV, G_GR = 0, 512, 1024, 2048
G_WIDTH = 3072


def _params(sem):
    return pltpu.CompilerParams(dimension_semantics=sem, vmem_limit_bytes=VMEM_LIMIT)


def _log_sigmoid(x):
    return jnp.minimum(x, 0.0) - jnp.log(1.0 + jnp.exp(-jnp.abs(x)))


def _silu(x):
    return x * (1.0 / (1.0 + jnp.exp(-x)))


def _rms(x, g):
    return x * lax.rsqrt(jnp.mean(x * x, axis=-1, keepdims=True) + EPS) * g


def _dot(a, b):
    return jnp.dot(a, b, preferred_element_type=F32)


def _dot_nt(a, b):
    return lax.dot_general(a, b, (((1,), (1,)), ((), ())), preferred_element_type=F32)


def _dot_tn(a, b):
    return lax.dot_general(a, b, (((0,), (0,)), ((), ())), preferred_element_type=F32)


def _split3(x):
    hi = x.astype(BF16).astype(F32)
    r1 = x - hi
    mid = r1.astype(BF16).astype(F32)
    lo = (r1 - mid).astype(BF16).astype(F32)
    return hi, mid, lo


def _norm_small_kernel(x_ref, g_ref, ws_ref, bs_ref, wa2_ref, ba_ref, bft_ref,
                       xn_ref, logf_ref, la_ref, c_ref, *rest, tm, seg, emit_cx):
    if emit_cx:
        cx_ref, carry_ref = rest
    else:
        (carry_ref,) = rest
    i = pl.program_id(0)
    xn = _rms(x_ref[...], g_ref[0]).astype(BF16)
    xn_ref[...] = xn
    zs = _dot_nt(xn, ws_ref[0])
    logf_ref[...] = _log_sigmoid(zs + bs_ref[0])[:, :FOX_HEADS]
    la_pre = _dot(zs.astype(BF16), wa2_ref[0]) + ba_ref[0]
    la_ref[...] = _log_sigmoid(la_pre) * (1.0 / GLA_NORMALIZER)

    pad_rows = -tm % LANES
    zs_sq = zs if pad_rows == 0 else jnp.concatenate([zs, jnp.zeros((pad_rows, LANES), F32)], axis=0)
    lft = _log_sigmoid(zs_sq.T[:FOX_HEADS, :tm] + bft_ref[0])
    lane = lax.broadcasted_iota(jnp.int32, (FOX_HEADS, tm), 1)
    pos = lane & (seg - 1) if seg < tm else lane
    c = lft
    d = 1
    while d < min(seg, tm):
        c = c + jnp.where(pos >= d, pltpu.roll(c, d, axis=1), 0.0)
        d *= 2
    if seg > tm:
        tiles_per_seq = seg // tm

        @pl.when(i % tiles_per_seq == 0)
        def _():
            carry_ref[...] = jnp.zeros_like(carry_ref)

        c = c + carry_ref[:, 0:1]
        carry_ref[...] = jnp.broadcast_to(c[:, tm - 1:tm], carry_ref.shape)
    c_ref[0] = c

    if emit_cx:
        c2 = c * (-LOG2E)
        stacked = jnp.concatenate(
            [c2, c2, c2, jnp.zeros((LANES - 3 * FOX_HEADS, tm), F32)], axis=0)
        hi, mid, lo = _split3(stacked)
        rowid = lax.broadcasted_iota(jnp.int32, stacked.shape, 0)
        pieces = jnp.where(rowid < FOX_HEADS, hi, jnp.where(rowid < 2 * FOX_HEADS, mid, lo))
        cx_ref[...] = pieces.T.astype(BF16)


def norm_small(x, gamma, w_small, b_small, wa2_pad, b_a, bf_t, *, layer, tm, seg, emit_cx):
    m = x.shape[0]
    n_tiles = m // tm
    if seg >= tm:
        tps = seg // tm
        c_shape = (m // seg, FOX_HEADS, seg)
        c_map = lambda i: (i // tps, 0, i % tps)
    else:
        c_shape = (n_tiles, FOX_HEADS, tm)
        c_map = lambda i: (i, 0, 0)
    lay = lambda i: (layer, 0, 0)
    out_specs = [
        pl.BlockSpec((tm, D_MODEL), lambda i: (i, 0)),
        pl.BlockSpec((tm, FOX_HEADS), lambda i: (i, 0)),
        pl.BlockSpec((tm, GLA_KEY_WIDTH), lambda i: (i, 0)),
        pl.BlockSpec((1, FOX_HEADS, tm), c_map),
    ]
    out_shape = [
        jax.ShapeDtypeStruct((m, D_MODEL), BF16),
        jax.ShapeDtypeStruct((m, FOX_HEADS), F32),
        jax.ShapeDtypeStruct((m, GLA_KEY_WIDTH), F32),
        jax.ShapeDtypeStruct(c_shape, F32),
    ]
    if emit_cx:
        out_specs.append(pl.BlockSpec((tm, LANES), lambda i: (i, 0)))
        out_shape.append(jax.ShapeDtypeStruct((m, LANES), BF16))
    return pl.pallas_call(
        functools.partial(_norm_small_kernel, tm=tm, seg=seg, emit_cx=emit_cx),
        grid=(n_tiles,),
        in_specs=[
            pl.BlockSpec((tm, D_MODEL), lambda i: (i, 0)),
            pl.BlockSpec((1, 1, D_MODEL), lay),
            pl.BlockSpec((1, LANES, D_MODEL), lay),
            pl.BlockSpec((1, 1, LANES), lay),
            pl.BlockSpec((1, LANES, GLA_KEY_WIDTH), lay),
            pl.BlockSpec((1, 1, GLA_KEY_WIDTH), lay),
            pl.BlockSpec((1, FOX_HEADS, 1), lay),
        ],
        out_specs=out_specs,
        out_shape=out_shape,
        scratch_shapes=[pltpu.VMEM((FOX_HEADS, LANES), F32)],
        compiler_params=_params(("arbitrary",)),
        name="norm_small",
    )(x, gamma, w_small, b_small, wa2_pad, b_a, bf_t)


def _mm_kernel(*refs, n_a, w_t, has_scale, has_res, has_stack_in, n_out):
    a_refs = refs[:n_a]
    w_refs = refs[n_a:2 * n_a]
    pos = 2 * n_a
    mm = _dot_nt if w_t else _dot
    acc = mm(a_refs[0][...], w_refs[0][0].astype(BF16))
    for a_ref, w_ref in zip(a_refs[1:], w_refs[1:]):
        acc = acc + mm(a_ref[...], w_ref[0].astype(BF16))
    if has_scale:
        acc = acc * refs[pos][...]
        pos += 1
    if has_res:
        acc = refs[pos][...] + acc
        pos += 1
    if has_stack_in:
        pos += 1
    for o_ref in refs[pos:pos + n_out]:
        o_ref[...] = acc.astype(o_ref.dtype)


def matmul(a_list, w_list, *, layer, out_dtypes, tm, tn, n, w_rows=None, w_col0=0, w_t=False,
           scale=None, res=None, stacked=None):
    m = a_list[0].shape[0]
    n_i = m // tm
    w_rows = w_rows or [0] * len(a_list)
    in_specs = [pl.BlockSpec((tm, a.shape[1]), lambda i, j: (i, 0)) for a in a_list]
    for a, w, r in zip(a_list, w_list, w_rows):
        if w_t:
            in_specs.append(pl.BlockSpec((1, tn, a.shape[1]), lambda i, j, r=r: (layer, w_col0 + j, r)))
        else:
            in_specs.append(pl.BlockSpec((1, a.shape[1], tn), lambda i, j, r=r: (layer, r, w_col0 + j)))
    args = list(a_list) + list(w_list)
    if scale is not None:
        in_specs.append(pl.BlockSpec((1, tn), lambda i, j: (0, j)))
        args.append(scale)
    if res is not None:
        in_specs.append(pl.BlockSpec((tm, tn), lambda i, j: (i, j)))
        args.append(res)
    out_specs = [pl.BlockSpec((tm, tn), lambda i, j: (i, j)) for _ in out_dtypes]
    out_shape = [jax.ShapeDtypeStruct((m, n), dt) for dt in out_dtypes]
    aliases = {}
    has_stack_in = False
    if stacked is not None:
        out_specs[0] = pl.BlockSpec((tm, tn), lambda i, j: (layer * n_i + i, j))
        out_shape[0] = jax.ShapeDtypeStruct((DEPTH * m, n), out_dtypes[0])
        if stacked[0] is not None:
            has_stack_in = True
            in_specs.append(pl.BlockSpec(memory_space=pl.ANY))
            aliases = {len(args): 0}
            args.append(stacked[0])
    outs = pl.pallas_call(
        functools.partial(_mm_kernel, n_a=len(a_list), w_t=w_t, has_scale=scale is not None,
                          has_res=res is not None, has_stack_in=has_stack_in, n_out=len(out_dtypes)),
        grid=(n_i, n // tn),
        in_specs=in_specs,
        out_specs=out_specs,
        out_shape=out_shape,
        input_output_aliases=aliases,
        compiler_params=_params(("parallel", "arbitrary")),
        name="matmul",
    )(*args)
    return outs


def _fox_prompt_kernel(q_ref, k_ref, cx_ref, v_ref, gn_ref, o_ref, kx_sc, vt_sc, *, seq, tq):
    h = pl.program_id(1)
    kx_sc[:, 0:FOX_DIM] = k_ref[...]
    kx_sc[:, FOX_DIM:2 * FOX_DIM] = cx_ref[...]
    vt_sc[0:FOX_DIM, :] = v_ref[...].astype(F32).T.astype(BF16)
    vt_sc[FOX_DIM:FOX_DIM + BF16_ROWS, :] = jnp.ones((BF16_ROWS, seq), BF16)

    lane = lax.broadcasted_iota(jnp.int32, (tq, FOX_DIM), 1)
    pick = (lane == h) | (lane == h + FOX_HEADS) | (lane == h + 2 * FOX_HEADS)
    onehot = jnp.where(pick, 1.0, 0.0).astype(BF16)
    key_i = lax.broadcasted_iota(jnp.int32, (tq, tq), 0)
    qry_i = lax.broadcasted_iota(jnp.int32, (tq, tq), 1)
    causal = key_i <= qry_i

    def scores(qi):
        lo, hi = qi * tq, (qi + 1) * tq
        qx = jnp.concatenate([q_ref[lo:hi, :], onehot], axis=1)
        s_diag = jnp.where(causal, _dot_nt(kx_sc[lo:hi, :], qx), NEG_BIG)
        s_off = _dot_nt(kx_sc[0:lo, :], qx) if qi > 0 else None
        return s_diag, s_off

    n_q = seq // tq
    ahead = scores(0)
    for qi in range(n_q):
        lo, hi = qi * tq, (qi + 1) * tq
        s_diag, s_off = ahead
        if qi + 1 < n_q:
            ahead = scores(qi + 1)
        m = jnp.max(s_diag, axis=0, keepdims=True)
        if qi > 0:
            m = jnp.maximum(m, jnp.max(s_off, axis=0, keepdims=True))
            p_off = jnp.exp2(s_off - m).astype(BF16)
            ot = _dot(vt_sc[:, 0:lo], p_off)
        p_diag = jnp.exp2(s_diag - m).astype(BF16)
        ot_d = _dot(vt_sc[:, lo:hi], p_diag)
        ot = ot + ot_d if qi > 0 else ot_d
        o = ot[0:FOX_DIM, :] * (1.0 / ot[FOX_DIM:FOX_DIM + 1, :])
        y = o * lax.rsqrt(jnp.mean(o * o, axis=0, keepdims=True) + EPS)
        o_ref[lo:hi, :] = (y.T * gn_ref[0]).astype(o_ref.dtype)


def fox_prompt(q, k16, cx, v16, fox_norm, *, layer, batch, seq):
    m = batch * seq
    return pl.pallas_call(
        functools.partial(_fox_prompt_kernel, seq=seq, tq=TQ),
        grid=(batch, FOX_HEADS),
        in_specs=[
            pl.BlockSpec((seq, FOX_DIM), lambda b, h: (b, h)),
            pl.BlockSpec((seq, FOX_DIM), lambda b, h: (b, h)),
            pl.BlockSpec((seq, LANES), lambda b, h: (b, 0)),
            pl.BlockSpec((seq, FOX_DIM), lambda b, h: (b, h)),
            pl.BlockSpec((1, 1, FOX_DIM), lambda b, h: (layer, 0, h)),
        ],
        out_specs=pl.BlockSpec((seq, FOX_DIM), lambda b, h: (b, h)),
        out_shape=jax.ShapeDtypeStruct((m, FOX_WIDTH), BF16),
        scratch_shapes=[pltpu.VMEM((seq, 2 * FOX_DIM), BF16),
                        pltpu.VMEM((FOX_DIM + BF16_ROWS, seq), BF16)],
        compiler_params=_params(("parallel", "arbitrary")),
        name="fox_prompt",
    )(q, k16, cx, v16, fox_norm)


def _fox_sample_kernel(pt_ref, q_ref, kn_ref, vn_ref, cn_ref, gn_ref, *rest, pages, t_new):
    k_refs = rest[:pages]
    v_refs = rest[pages:2 * pages]
    lf_refs = rest[2 * pages:3 * pages]
    o_ref, m_sc, l_sc, acc_sc, carry_sc = rest[3 * pages:]
    del pt_ref
    step = pl.program_id(1)
    n_steps = pl.num_programs(1)
    rows = t_new

    @pl.when(step == 0)
    def _():
        m_sc[...] = jnp.full_like(m_sc, NEG_BIG)
        l_sc[...] = jnp.zeros_like(l_sc)
        acc_sc[...] = jnp.zeros_like(acc_sc)
        carry_sc[...] = jnp.zeros_like(carry_sc)

    flat = PAGE * FOX_HEADS
    lane = lax.broadcasted_iota(jnp.int32, (pages, flat), 1)
    lf = jnp.concatenate([lf_refs[p][0, 0] for p in range(pages)], axis=0)
    suf, tot = lf, lf
    d = FOX_HEADS
    while d < flat:
        suf = suf + jnp.where(lane < flat - d, pltpu.roll(suf, flat - d, axis=1), 0.0)
        tot = tot + pltpu.roll(tot, d, axis=1)
        d *= 2
    run = carry_sc[0:1, :]
    bias = [None] * pages
    for p in range(pages - 1, -1, -1):
        bias[p] = ((suf[p:p + 1, :] - lf[p:p + 1, :]) + run) * LOG2E
        run = run + tot[p:p + 1, :]
    carry_sc[0:1, :] = run

    n_rows = FOX_HEADS * rows
    q_flat = jnp.concatenate(
        [q_ref[:, h * FOX_DIM:(h + 1) * FOX_DIM] for h in range(FOX_HEADS)], axis=0).astype(BF16)
    zeros = jnp.zeros_like(q_flat)
    q_pair = jnp.concatenate([jnp.concatenate([q_flat, zeros], axis=1),
                              jnp.concatenate([zeros, q_flat], axis=1)], axis=0)
    row_i = lax.broadcasted_iota(jnp.int32, (n_rows, flat), 0)
    col_i = lax.broadcasted_iota(jnp.int32, (n_rows, flat), 1)
    own = (row_i >> (rows.bit_length() - 1)) == (col_i & (FOX_HEADS - 1))

    def local_softmax(s, v_all):
        m = jnp.max(s, axis=-1, keepdims=True)
        pr = jnp.exp2(s - m)
        return m, jnp.sum(pr, axis=-1, keepdims=True), _dot(pr.astype(BF16), v_all)

    def merge(groups):
        m_old = m_sc[...]
        m_new = m_old
        for m, _, _ in groups:
            m_new = jnp.maximum(m_new, m)
        alpha = jnp.exp2(m_old - m_new)
        l_new = alpha * l_sc[...]
        acc = alpha * acc_sc[...]
        for m, l, o in groups:
            w = jnp.exp2(m - m_new)
            l_new = l_new + w * l
            acc = acc + w * o
        m_sc[...] = m_new
        l_sc[...] = l_new
        acc_sc[...] = acc

    s_parts = [None] * pages
    for a in range(0, pages, 2):
        kk = jnp.concatenate([k_refs[a][0, 0], k_refs[a + 1][0, 0]], axis=1).astype(BF16)
        s2 = _dot_nt(q_pair, kk)
        s_parts[a] = jnp.where(own, s2[0:n_rows] + bias[a], NEG_BIG)
        s_parts[a + 1] = jnp.where(own, s2[n_rows:2 * n_rows] + bias[a + 1], NEG_BIG)
    half = pages // 2
    groups = []
    for lo_p, hi_p in ((0, half), (half, pages)):
        groups.append(local_softmax(
            jnp.concatenate(s_parts[lo_p:hi_p], axis=1),
            jnp.concatenate([v_refs[p][0, 0].astype(BF16) for p in range(lo_p, hi_p)], axis=0)))
    merge(groups)

    @pl.when(step == n_steps - 1)
    def _():
        valid = own & ((col_i >> (FOX_HEADS.bit_length() - 1)) <= (row_i & (rows - 1)))
        s_new = _dot_nt(q_flat, kn_ref[...]) - cn_ref[0] * LOG2E
        merge([local_softmax(jnp.where(valid, s_new, NEG_BIG), vn_ref[...])])
        for h in range(FOX_HEADS):
            cols = slice(h * FOX_DIM, (h + 1) * FOX_DIM)
            sl = slice(h * rows, (h + 1) * rows)
            o = acc_sc[sl, :] / l_sc[sl, :]
            o_ref[:, cols] = _rms(o, gn_ref[0, :, cols])


def fox_sample(page_table, q, k_new_page, v_new_page, c_new_flat, fox_norm, cache_k4, cache_v4,
               cache_lf_flat, *, layer, pages):
    db, n_pages = page_table.shape
    t_new = q.shape[0] // db
    assert t_new == SUBLANES and pages % 2 == 0 and n_pages % pages == 0
    n_steps = n_pages // pages
    flat = PAGE * FOX_HEADS
    pt_flat = page_table.reshape(-1)

    def page_map(p):
        def index(b, s, pt):
            return (layer, pt[b * n_pages + (n_steps - 1 - s) * pages + p], 0, 0)
        return index

    in_specs = [
        pl.BlockSpec((t_new, FOX_WIDTH), lambda b, s, pt: (b, 0)),
        pl.BlockSpec((flat, FOX_DIM), lambda b, s, pt: (b, 0)),
        pl.BlockSpec((flat, FOX_DIM), lambda b, s, pt: (b, 0)),
        pl.BlockSpec((1, 1, flat), lambda b, s, pt: (b, 0, 0)),
        pl.BlockSpec((1, 1, FOX_WIDTH), lambda b, s, pt: (layer, 0, 0)),
    ]
    in_specs += [pl.BlockSpec((1, 1, flat, FOX_DIM), page_map(p)) for p in range(pages)]
    in_specs += [pl.BlockSpec((1, 1, flat, FOX_DIM), page_map(p)) for p in range(pages)]
    in_specs += [pl.BlockSpec((1, 1, 1, flat), page_map(p)) for p in range(pages)]
    grid_spec = pltpu.PrefetchScalarGridSpec(
        num_scalar_prefetch=1,
        grid=(db, n_steps),
        in_specs=in_specs,
        out_specs=pl.BlockSpec((t_new, FOX_WIDTH), lambda b, s, pt: (b, 0)),
        scratch_shapes=[pltpu.VMEM((FOX_HEADS * t_new, 1), F32),
                        pltpu.VMEM((FOX_HEADS * t_new, 1), F32),
                        pltpu.VMEM((FOX_HEADS * t_new, FOX_DIM), F32),
                        pltpu.VMEM((SUBLANES, flat), F32)],
    )
    return pl.pallas_call(
        functools.partial(_fox_sample_kernel, pages=pages, t_new=t_new),
        grid_spec=grid_spec,
        out_shape=jax.ShapeDtypeStruct((db * t_new, FOX_WIDTH), F32),
        compiler_params=_params(("parallel", "arbitrary")),
        name="fox_sample",
    )(pt_flat, q, k_new_page, v_new_page, c_new_flat, fox_norm,
      *([cache_k4] * pages), *([cache_v4] * pages), *([cache_lf_flat] * pages))


def _gla_kernel(q_ref, k_ref, v_ref, la_ref, r_ref, gn_ref, s0_ref, o_ref, sfin_ref,
                s_sc, b_sc, *, chunk, sub, hb):
    ci = pl.program_id(2)
    n_chunks = pl.num_programs(2)
    heads = range(hb)
    kcols = lambda h: slice(h * GLA_DK, (h + 1) * GLA_DK)
    vcols = lambda h: slice(h * GLA_DV, (h + 1) * GLA_DV)

    @pl.when(ci == 0)
    def _():
        s_sc[...] = s0_ref[0]

    q = [q_ref[:, kcols(h)].astype(F32) for h in heads]
    k = [k_ref[:, kcols(h)].astype(F32) for h in heads]
    v = [v_ref[:, vcols(h)] for h in heads]
    rowi = lax.broadcasted_iota(jnp.int32, (chunk, chunk), 0)
    coli = lax.broadcasted_iota(jnp.int32, (chunk, chunk), 1)
    row = lax.broadcasted_iota(jnp.int32, (chunk, 1), 0)

    tri = jnp.where(rowi >= coli, 1.0, 0.0).astype(BF16)
    hi, mid, lo = _split3(la_ref[...])
    b_all = _dot(tri, hi.astype(BF16)) + _dot(tri, mid.astype(BF16)) + _dot(tri, lo.astype(BF16))
    b_sc[...] = b_all
    b = [b_all[:, kcols(h)] for h in heads]
    b_last = [b_sc[chunk - 1:chunk, kcols(h)] for h in heads]

    ones = jnp.ones((GLA_DK, chunk), BF16)
    diff = rowi - coli
    sub_shift = sub.bit_length() - 1
    a = [jnp.zeros((chunk, chunk), F32) for _ in heads]
    for d in range(sub):
        for h in heads:
            k_s = k[h] if d == 0 else pltpu.roll(k[h], d, axis=0)
            b_s = b[h] if d == 0 else pltpu.roll(b[h], d, axis=0)
            prod = q[h] * k_s * jnp.exp(b[h] - b_s)
            band = _dot(prod.astype(BF16), ones)
            a[h] = jnp.where(diff == d, band, a[h])
    in_sub = (rowi >> sub_shift) == (coli >> sub_shift)
    a = [jnp.where(in_sub, a[h], 0.0) for h in heads]

    m = sub
    while m < chunk:
        groups = chunk // (2 * m)
        odd = (row & m) != 0
        shift = (2 * m).bit_length() - 1
        same = (rowi >> shift) == (coli >> shift)
        for h in heads:
            ref_rows = [jnp.broadcast_to(b_sc[g * 2 * m + m - 1:g * 2 * m + m, kcols(h)], (2 * m, GLA_DK))
                        for g in range(groups)]
            ref_b = ref_rows[0] if groups == 1 else jnp.concatenate(ref_rows, axis=0)
            e = jnp.exp(-jnp.abs(b[h] - ref_b))
            qm = jnp.where(odd, q[h] * e, 0.0).astype(BF16)
            km = jnp.where(odd, 0.0, k[h] * e).astype(BF16)
            a[h] = a[h] + jnp.where(same, _dot_nt(qm, km), 0.0)
        m *= 2

    s_old = [s_sc[h] for h in heads]
    o = [_dot((q[h] * jnp.exp(b[h])).astype(BF16), s_old[h].astype(BF16))
         + _dot(a[h].astype(BF16), v[h]) for h in heads]

    s_new = []
    for h in heads:
        kl = (k[h] * jnp.exp(b_last[h] - b[h])).astype(BF16)
        decay = jnp.broadcast_to(jnp.exp(b_last[h]), (GLA_DK, GLA_DK)).T
        decay = jnp.concatenate([decay, decay], axis=1)
        s_new.append(decay * s_old[h] + _dot_tn(kl, v[h]))
        s_sc[h] = s_new[h]

    @pl.when(ci == n_chunks - 1)
    def _():
        for h in heads:
            sfin_ref[0, h] = s_new[h]

    for h in heads:
        r = r_ref[:, vcols(h)].astype(F32)
        o_ref[:, vcols(h)] = (_rms(o[h], gn_ref[0, :, vcols(h)]) * _silu(r)).astype(o_ref.dtype)


def gla(gp, la, gla_norm, s0, *, layer, batch, seq):
    m = batch * seq
    chunk = GLA_CHUNK
    hb = GLA_HEAD_BLOCK
    kw, vw = hb * GLA_DK, hb * GLA_DV
    n_chunks = seq // chunk
    rows = lambda b, h, c: b * n_chunks + c
    return pl.pallas_call(
        functools.partial(_gla_kernel, chunk=chunk, sub=GLA_SUB, hb=hb),
        grid=(batch, GLA_HEADS // hb, n_chunks),
        in_specs=[
            pl.BlockSpec((chunk, kw), lambda b, h, c: (rows(b, h, c), G_GQ // kw + h)),
            pl.BlockSpec((chunk, kw), lambda b, h, c: (rows(b, h, c), G_GK // kw + h)),
            pl.BlockSpec((chunk, vw), lambda b, h, c: (rows(b, h, c), G_GV // vw + h)),
            pl.BlockSpec((chunk, kw), lambda b, h, c: (rows(b, h, c), h)),
            pl.BlockSpec((chunk, vw), lambda b, h, c: (rows(b, h, c), G_GR // vw + h)),
            pl.BlockSpec((1, 1, vw), lambda b, h, c: (layer, 0, h)),
            pl.BlockSpec((1, hb, GLA_DK, GLA_DV), lambda b, h, c: (b, h, 0, 0)),
        ],
        out_specs=[
            pl.BlockSpec((chunk, vw), lambda b, h, c: (rows(b, h, c), h)),
            pl.BlockSpec((1, hb, GLA_DK, GLA_DV), lambda b, h, c: (b, h, 0, 0)),
        ],
        out_shape=[
            jax.ShapeDtypeStruct((m, GLA_WIDTH), BF16),
            jax.ShapeDtypeStruct((batch, GLA_HEADS, GLA_DK, GLA_DV), F32),
        ],
        scratch_shapes=[pltpu.VMEM((hb, GLA_DK, GLA_DV), F32), pltpu.VMEM((chunk, kw), F32)],
        compiler_params=_params(("parallel", "parallel", "arbitrary")),
        name="gla",
    )(gp, gp, gp, la, gp, gla_norm, s0)


def _conv_gate(g, g1, g2, u, cw_ref, cb_ref, cols):
    conv = (cb_ref[0, :, cols] + cw_ref[0, 0:1, cols] * g2 + cw_ref[0, 1:2, cols] * g1
            + cw_ref[0, 2:3, cols] * g)
    return _silu(conv) * u


def _ffn_up_prompt_kernel(x_ref, halo_ref, g_ref, wg_ref, wu_ref, cw_ref, cb_ref,
                          h_ref, gtail_ref, xn_sc, *, tm, tn, tiles_per_seq):
    i = pl.program_id(0)
    j = pl.program_id(1)

    @pl.when(j == 0)
    def _():
        xn_sc[0:tm, :] = _rms(x_ref[...], g_ref[0]).astype(BF16)
        keep = jnp.where(i % tiles_per_seq == 0, 0.0, 1.0)
        xn_sc[tm:tm + BF16_ROWS, :] = (_rms(halo_ref[...], g_ref[0]) * keep).astype(BF16)

    for t in range(tn // MXU_COLS):
        cols = slice(t * MXU_COLS, (t + 1) * MXU_COLS)
        g_ext = _dot(xn_sc[...], wg_ref[0, :, cols].astype(BF16))
        u = _dot(xn_sc[0:tm, :], wu_ref[0, :, cols].astype(BF16))
        g = g_ext[0:tm]
        g1 = pltpu.roll(g_ext, 1, axis=0)[0:tm]
        g2 = pltpu.roll(g_ext, 2, axis=0)[0:tm]
        h_ref[:, cols] = _conv_gate(g, g1, g2, u, cw_ref, cb_ref, cols).astype(h_ref.dtype)
        gtail_ref[:, cols] = g[tm - SUBLANES:tm]


def ffn_up_prompt(x, gamma, wg, wu, conv_w, conv_b, *, layer, seq, tm, tn):
    m = x.shape[0]
    tiles_per_seq = seq // tm
    halo_blocks = tm // BF16_ROWS
    return pl.pallas_call(
        functools.partial(_ffn_up_prompt_kernel, tm=tm, tn=tn, tiles_per_seq=tiles_per_seq),
        grid=(m // tm, D_FF // tn),
        in_specs=[
            pl.BlockSpec((tm, D_MODEL), lambda i, j: (i, 0)),
            pl.BlockSpec((BF16_ROWS, D_MODEL), lambda i, j: (jnp.maximum(i * halo_blocks - 1, 0), 0)),
            pl.BlockSpec((1, 1, D_MODEL), lambda i, j: (layer, 0, 0)),
            pl.BlockSpec((1, D_MODEL, tn), lambda i, j: (layer, 0, j)),
            pl.BlockSpec((1, D_MODEL, tn), lambda i, j: (layer, 0, j)),
            pl.BlockSpec((1, CONV_W, tn), lambda i, j: (layer, 0, j)),
            pl.BlockSpec((1, 1, tn), lambda i, j: (layer, 0, j)),
        ],
        out_specs=[
            pl.BlockSpec((tm, tn), lambda i, j: (i, j)),
            pl.BlockSpec((SUBLANES, tn), lambda i, j: (i, j)),
        ],
        out_shape=[
            jax.ShapeDtypeStruct((m, D_FF), BF16),
            jax.ShapeDtypeStruct((m // tm * SUBLANES, D_FF), F32),
        ],
        scratch_shapes=[pltpu.VMEM((tm + BF16_ROWS, D_MODEL), BF16)],
        compiler_params=_params(("parallel", "arbitrary")),
        name="ffn_up_prompt",
    )(x, x, gamma, wg, wu, conv_w, conv_b)


def _ffn_up_sample_kernel(x_ref, g_ref, wg_ref, wu_ref, cw_ref, cb_ref, e1_ref, e2_ref,
                          h_ref, gout_ref, xn_sc, *, seq, tn):
    j = pl.program_id(0)

    @pl.when(j == 0)
    def _():
        xn_sc[...] = _rms(x_ref[...], g_ref[0]).astype(BF16)

    cols = slice(0, tn)
    g = _dot(xn_sc[...], wg_ref[0].astype(BF16))
    u = _dot(xn_sc[...], wu_ref[0].astype(BF16))
    pos = lax.broadcasted_iota(jnp.int32, (g.shape[0], 1), 0) & (seq - 1)
    g1 = jnp.where(pos >= 1, pltpu.roll(g, 1, axis=0), e1_ref[...])
    g2 = jnp.where(pos >= 2, pltpu.roll(g, 2, axis=0), e2_ref[...])
    h_ref[...] = _conv_gate(g, g1, g2, u, cw_ref, cb_ref, cols).astype(h_ref.dtype)
    gout_ref[...] = g


def ffn_up_sample(x, gamma, wg, wu, conv_w, conv_b, e1, e2, *, layer, seq, tn):
    m = x.shape[0]
    return pl.pallas_call(
        functools.partial(_ffn_up_sample_kernel, seq=seq, tn=tn),
        grid=(D_FF // tn,),
        in_specs=[
            pl.BlockSpec((m, D_MODEL), lambda j: (0, 0)),
            pl.BlockSpec((1, 1, D_MODEL), lambda j: (layer, 0, 0)),
            pl.BlockSpec((1, D_MODEL, tn), lambda j: (layer, 0, j)),
            pl.BlockSpec((1, D_MODEL, tn), lambda j: (layer, 0, j)),
            pl.BlockSpec((1, CONV_W, tn), lambda j: (layer, 0, j)),
            pl.BlockSpec((1, 1, tn), lambda j: (layer, 0, j)),
            pl.BlockSpec((m, tn), lambda j: (0, j)),
            pl.BlockSpec((m, tn), lambda j: (0, j)),
        ],
        out_specs=[
            pl.BlockSpec((m, tn), lambda j: (0, j)),
            pl.BlockSpec((m, tn), lambda j: (0, j)),
        ],
        out_shape=[
            jax.ShapeDtypeStruct((m, D_FF), BF16),
            jax.ShapeDtypeStruct((m, D_FF), F32),
        ],
        scratch_shapes=[pltpu.VMEM((m, D_MODEL), BF16)],
        compiler_params=_params(("arbitrary",)),
        name="ffn_up_sample",
    )(x, gamma, wg, wu, conv_w, conv_b, e1, e2)


def _rmsnorm_kernel(x_ref, g_ref, o_ref):
    o_ref[...] = _rms(x_ref[...], g_ref[...])


def rmsnorm(x, gamma, *, tm):
    m = x.shape[0]
    return pl.pallas_call(
        _rmsnorm_kernel,
        grid=(m // tm,),
        in_specs=[pl.BlockSpec((tm, D_MODEL), lambda i: (i, 0)),
                  pl.BlockSpec((1, D_MODEL), lambda i: (0, 0))],
        out_specs=pl.BlockSpec((tm, D_MODEL), lambda i: (i, 0)),
        out_shape=jax.ShapeDtypeStruct((m, D_MODEL), F32),
        compiler_params=_params(("parallel",)),
        name="rmsnorm",
    )(x, gamma)


def _window_cast_kernel(a_ref, b_ref, o_ref, *, shift):
    x = jnp.concatenate([a_ref[0], b_ref[0]], axis=0)
    o_ref[0] = x[shift:, :].astype(o_ref.dtype)


def window_cast(w_t, *, row0, width, rows):
    depth, _, k_dim = w_t.shape
    blk, shift = divmod(row0, rows)
    assert shift == SUBLANES and width % rows == 0
    return pl.pallas_call(
        functools.partial(_window_cast_kernel, shift=shift),
        grid=(depth, width // rows),
        in_specs=[pl.BlockSpec((1, rows, k_dim), lambda l, i: (l, blk + i, 0)),
                  pl.BlockSpec((1, shift, k_dim), lambda l, i: (l, (blk + i + 1) * (rows // shift), 0))],
        out_specs=pl.BlockSpec((1, rows, k_dim), lambda l, i: (l, i, 0)),
        out_shape=jax.ShapeDtypeStruct((depth, width, k_dim), BF16),
        compiler_params=_params(("parallel", "parallel")),
        name="window_cast",
    )(w_t, w_t)


def _prep_weights(w_in_t, b_f, w_a2, b_a, w_out, w_gate, w_up, w_down):
    o_fl = 3 * FOX_WIDTH
    o_gq = o_fl + FOX_HEADS
    o_ga = o_gq + 2 * GLA_KEY_WIDTH + 2 * GLA_WIDTH
    w_fl, w_ga = lax.optimization_barrier(
        (w_in_t[:, o_fl:o_fl + FOX_HEADS, :], w_in_t[:, o_ga:o_ga + GLA_RANK, :]))
    pad = LANES - FOX_HEADS - GLA_RANK
    zeros = lambda *s: jnp.zeros((DEPTH,) + s, F32)
    return dict(
        w_g=window_cast(w_in_t, row0=o_gq, width=G_WIDTH, rows=256),
        w_small=jnp.concatenate([w_fl, w_ga, zeros(pad, D_MODEL)], axis=1).astype(BF16),
        b_small=jnp.concatenate([b_f, zeros(LANES - FOX_HEADS)], axis=1).reshape(DEPTH, 1, LANES),
        wa2_pad=jnp.concatenate([zeros(FOX_HEADS, GLA_KEY_WIDTH), w_a2, zeros(pad, GLA_KEY_WIDTH)],
                                axis=1).astype(BF16),
        b_a=b_a.reshape(DEPTH, 1, GLA_KEY_WIDTH),
        bf_t=b_f.reshape(DEPTH, FOX_HEADS, 1),
        w_out=w_out.astype(BF16), w_gate=w_gate, w_up=w_up,
        w_down=w_down.astype(BF16),
    )


def _pad_rows(x, batch, t, t_pad):
    w = x.shape[-1]
    return jnp.pad(x.reshape(batch, t, w), ((0, 0), (0, t_pad - t), (0, 0))).reshape(batch * t_pad, w)


def kernel(x_prompt, x_sample, cache_k, cache_v, cache_logf, state_gla, state_conv, page_table,
           w_in, b_f, w_a2, b_a, fox_norm, gla_norm, w_out, norm_attn, norm_ffn,
           w_gate, w_up, conv_w, conv_b, w_down, norm_final):
    bp, seq, _ = x_prompt.shape
    db, t_new, _ = x_sample.shape
    n_pool = cache_k.shape[1]
    mp, ms = bp * seq, db * t_new

    xp = x_prompt.reshape(mp, D_MODEL)
    xs = x_sample.reshape(ms, D_MODEL)
    cache_k4 = cache_k.reshape(DEPTH, n_pool, PAGE * FOX_HEADS, FOX_DIM)
    cache_v4 = cache_v.reshape(DEPTH, n_pool, PAGE * FOX_HEADS, FOX_DIM)
    cache_lf_flat = cache_logf.reshape(DEPTH, n_pool, 1, PAGE * FOX_HEADS)
    q_scale = jnp.full((1, FOX_WIDTH), FOX_DIM ** -0.5 * LOG2E, F32)
    g_scale = jnp.concatenate([jnp.full((GLA_KEY_WIDTH,), GLA_DK ** -0.5, F32),
                               jnp.ones((G_WIDTH - GLA_KEY_WIDTH,), F32)]).reshape(1, G_WIDTH)
    zero_state = jnp.zeros((bp, GLA_HEADS, GLA_DK, GLA_DV), F32)
    w_in_t = jnp.swapaxes(w_in, 1, 2)
    w = _prep_weights(w_in_t, b_f, w_a2, b_a, w_out, w_gate, w_up, w_down)
    conv_b3 = conv_b.reshape(DEPTH, 1, D_FF)
    norm_attn = norm_attn.reshape(DEPTH, 1, D_MODEL)
    norm_ffn = norm_ffn.reshape(DEPTH, 1, D_MODEL)
    fox_norm = fox_norm.reshape(DEPTH, 1, FOX_WIDTH)
    gla_norm = gla_norm.reshape(DEPTH, 1, GLA_WIDTH)
    small = (w["w_small"], w["b_small"], w["wa2_pad"], w["b_a"], w["bf_t"])

    def projections(xn, l, tm, k_stack, v_stack):
        wide = 2 * TN
        (q,) = matmul([xn], [w_in_t], layer=l, out_dtypes=[BF16], tm=tm, tn=wide, n=FOX_WIDTH, w_col0=0,
                      w_t=True, scale=q_scale)
        k_stack, k16 = matmul([xn], [w_in_t], layer=l, out_dtypes=[F32, BF16], tm=tm, tn=wide, n=FOX_WIDTH,
                              w_col0=1, w_t=True, stacked=(k_stack,))
        v_stack, v16 = matmul([xn], [w_in_t], layer=l, out_dtypes=[F32, BF16], tm=tm, tn=wide, n=FOX_WIDTH,
                              w_col0=2, w_t=True, stacked=(v_stack,))
        (gp,) = matmul([xn], [w["w_g"]], layer=l, out_dtypes=[BF16], tm=tm, tn=wide, n=G_WIDTH, w_t=True,
                       scale=g_scale)
        return q, k_stack, k16, v_stack, v16, gp

    def out_and_down(fo, go, h_fn, x, l, tm_out, tm):
        (x,) = matmul([fo, go], [w["w_out"], w["w_out"]], layer=l, out_dtypes=[F32], tm=tm_out, tn=D_MODEL,
                      n=D_MODEL, w_rows=[0, 1], res=x)
        h, extra = h_fn(x)
        (x,) = matmul([h], [w["w_down"]], layer=l, out_dtypes=[F32], tm=tm, tn=TN, n=D_MODEL, res=x)
        return x, extra

    outs = {k: [] for k in ("lp", "sp", "cp", "ls", "ss", "cs")}
    kp = vp = ksm = vsm = None
    for l in range(DEPTH):
        xn, logf, la, _, cx = norm_small(xp, norm_attn, *small, layer=l, tm=TM, seg=seq, emit_cx=True)
        q, kp, k16, vp, v16, gp = projections(xn, l, TM, kp, vp)
        fo = fox_prompt(q, k16, cx, v16, fox_norm, layer=l, batch=bp, seq=seq)
        go, s_fin = gla(gp, la, gla_norm, zero_state, layer=l, batch=bp, seq=seq)
        ffn = lambda x: ffn_up_prompt(x, norm_ffn, w["w_gate"], w["w_up"], conv_w, conv_b3,
                                      layer=l, seq=seq, tm=TM, tn=TN)
        xp, gtail = out_and_down(fo, go, ffn, xp, l, TM // 2, TM)
        outs["lp"].append(logf.reshape(bp, seq, FOX_HEADS))
        outs["sp"].append(s_fin)
        outs["cp"].append(gtail.reshape(bp, seq // TM, SUBLANES, D_FF)[:, -1, SUBLANES - (CONV_W - 1):, :])

        xn, logf, la, c = norm_small(xs, norm_attn, *small, layer=l, tm=ms, seg=t_new, emit_cx=False)
        q, ksm, k16, vsm, v16, gp = projections(xn, l, ms, ksm, vsm)
        kn_page = _pad_rows(k16, db, t_new, PAGE).reshape(db * PAGE * FOX_HEADS, FOX_DIM)
        vn_page = _pad_rows(v16, db, t_new, PAGE).reshape(db * PAGE * FOX_HEADS, FOX_DIM)
        c_new = c.reshape(FOX_HEADS, db, t_new).transpose(1, 2, 0)
        c_new_flat = jnp.pad(c_new, ((0, 0), (0, PAGE - t_new), (0, 0))).reshape(db, 1, PAGE * FOX_HEADS)
        fo = fox_sample(page_table, q.astype(F32), kn_page, vn_page, c_new_flat, fox_norm,
                        cache_k4, cache_v4, cache_lf_flat, layer=l, pages=SAMPLE_PAGES)
        gp_pad = _pad_rows(gp, db, t_new, GLA_CHUNK)
        la_pad = _pad_rows(la, db, t_new, GLA_CHUNK)
        go_pad, s_fin = gla(gp_pad, la_pad, gla_norm, state_gla[l], layer=l, batch=db, seq=GLA_CHUNK)
        go = go_pad.reshape(db, GLA_CHUNK, GLA_WIDTH)[:, :t_new].reshape(ms, GLA_WIDTH)
        st = state_conv[l]
        zeros_f = jnp.zeros((db, t_new - 1, D_FF), F32)
        e1 = jnp.concatenate([st[:, 1:2], zeros_f], axis=1).reshape(ms, D_FF)
        e2 = jnp.concatenate([st[:, 0:2], zeros_f[:, 1:]], axis=1).reshape(ms, D_FF)
        ffn = lambda x: ffn_up_sample(x, norm_ffn, w["w_gate"], w["w_up"], conv_w, conv_b3, e1, e2,
                                      layer=l, seq=t_new, tn=TN)
        xs, g_all = out_and_down(fo.astype(BF16), go, ffn, xs, l, ms, ms)
        outs["ls"].append(logf.reshape(db, t_new, FOX_HEADS))
        outs["ss"].append(s_fin)
        outs["cs"].append(g_all.reshape(db, t_new, D_FF)[:, t_new - (CONV_W - 1):, :])

    g_fin = norm_final.reshape(1, D_MODEL)
    y_prompt = rmsnorm(xp, g_fin, tm=TM).reshape(bp, seq, D_MODEL)
    y_sample = rmsnorm(xs, g_fin, tm=ms).reshape(db, t_new, D_MODEL)
    stack = lambda key: jnp.stack(outs[key])
    kv_p = lambda t: t.reshape(DEPTH, bp, seq, FOX_HEADS, FOX_DIM)
    kv_s = lambda t: t.reshape(DEPTH, db, t_new, FOX_HEADS, FOX_DIM)
    return (y_prompt, y_sample,
            kv_p(kp), kv_p(vp), stack("lp"), stack("sp"), stack("cp"),
            kv_s(ksm), kv_s(vsm), stack("ls"), stack("ss"), stack("cs"))
```

```python
import functools
import math

import jax
import jax.numpy as jnp
from jax import lax
from jax.experimental import pallas as pl
from jax.experimental.pallas import tpu as pltpu

F32 = jnp.float32
BF16 = jnp.bfloat16

D_MODEL = 2048
DEPTH = 4
PAGE = 128
FOX_HEADS = 8
FOX_DIM = 128
FOX_WIDTH = FOX_HEADS * FOX_DIM
GLA_HEADS = 4
GLA_DK = 128
GLA_DV = 256
GLA_KEY_WIDTH = GLA_HEADS * GLA_DK
GLA_WIDTH = GLA_HEADS * GLA_DV
GLA_RANK = 16
GLA_NORMALIZER = 16.0
D_FF = 5632
CONV_W = 3
EPS = 1e-6
LOG2E = math.log2(math.e)

LANES = 128
SUBLANES = 8
BF16_ROWS = 16
MXU_COLS = 256
VMEM_LIMIT = 56 * 1024 * 1024

GLA_CHUNK = 128
GLA_SUB = 8
GLA_HEAD_BLOCK = 4
NEG_BIG = -1e30

TM = 1024
TN = 512
TQ = 512
SAMPLE_PAGES = 16

G_GQ, G_GK, G_GV, G_GR = 0, 512, 1024, 2048
G_WIDTH = 3072


def _params(sem):
    return pltpu.CompilerParams(dimension_semantics=sem, vmem_limit_bytes=VMEM_LIMIT)


def _log_sigmoid(x):
    return jnp.minimum(x, 0.0) - jnp.log(1.0 + jnp.exp(-jnp.abs(x)))


def _silu(x):
    return (0.5 * x) * (1.0 + jnp.tanh(0.5 * x))


def _rms(x, g):
    return x * lax.rsqrt(jnp.mean(x * x, axis=-1, keepdims=True) + EPS) * g


def _dot(a, b):
    return jnp.dot(a, b, preferred_element_type=F32)


def _dot_nt(a, b):
    return lax.dot_general(a, b, (((1,), (1,)), ((), ())), preferred_element_type=F32)


def _dot_tn(a, b):
    return lax.dot_general(a, b, (((0,), (0,)), ((), ())), preferred_element_type=F32)


def _split3(x):
    hi = x.astype(BF16).astype(F32)
    r1 = x - hi
    mid = r1.astype(BF16).astype(F32)
    lo = (r1 - mid).astype(BF16).astype(F32)
    return hi, mid, lo


def _norm_small_kernel(x_ref, g_ref, ws_ref, wa2_ref, ba_ref, bft_ref,
                       xn_ref, logf_ref, la_ref, c_ref, *rest, tm, seg, emit_cx):
    if emit_cx:
        cx_ref, carry_ref = rest
    else:
        (carry_ref,) = rest
    i = pl.program_id(0)
    xn = _rms(x_ref[...], g_ref[0]).astype(BF16)
    xn_ref[...] = xn
    zs = _dot_nt(xn, ws_ref[0])
    la_pre = _dot(zs.astype(BF16), wa2_ref[0]) + ba_ref[0]
    la_ref[...] = _log_sigmoid(la_pre) * (1.0 / GLA_NORMALIZER)

    pad_rows = -tm % LANES
    zs_sq = zs if pad_rows == 0 else jnp.concatenate([zs, jnp.zeros((pad_rows, LANES), F32)], axis=0)
    lft = _log_sigmoid(zs_sq.T[:FOX_HEADS, :tm] + bft_ref[0])
    logf_ref[0] = lft
    lane = lax.broadcasted_iota(jnp.int32, (FOX_HEADS, tm), 1)
    pos = lane & (seg - 1) if seg < tm else lane
    c = lft
    d = 1
    while d < min(seg, tm):
        c = c + jnp.where(pos >= d, pltpu.roll(c, d, axis=1), 0.0)
        d *= 2
    if seg > tm:
        tiles_per_seq = seg // tm

        @pl.when(i % tiles_per_seq == 0)
        def _():
            carry_ref[...] = jnp.zeros_like(carry_ref)

        c = c + carry_ref[:, 0:1]
        carry_ref[...] = jnp.broadcast_to(c[:, tm - 1:tm], carry_ref.shape)
    c_ref[0] = c

    if emit_cx:
        c2 = c * (-LOG2E)
        stacked = jnp.concatenate(
            [c2, c2, c2, jnp.zeros((LANES - 3 * FOX_HEADS, tm), F32)], axis=0)
        hi, mid, lo = _split3(stacked)
        rowid = lax.broadcasted_iota(jnp.int32, stacked.shape, 0)
        pieces = jnp.where(rowid < FOX_HEADS, hi, jnp.where(rowid < 2 * FOX_HEADS, mid, lo))
        cx_ref[...] = pieces.T.astype(BF16)


def norm_small(x, gamma, w_small, wa2_pad, b_a, bf_t, *, layer, tm, seg, emit_cx):
    m = x.shape[0]
    n_tiles = m // tm
    if seg >= tm:
        tps = seg // tm
        c_shape = (m // seg, FOX_HEADS, seg)
        c_map = lambda i: (i // tps, 0, i % tps)
    else:
        c_shape = (n_tiles, FOX_HEADS, tm)
        c_map = lambda i: (i, 0, 0)
    lay = lambda i: (layer, 0, 0)
    out_specs = [
        pl.BlockSpec((tm, D_MODEL), lambda i: (i, 0)),
        pl.BlockSpec((1, FOX_HEADS, tm), c_map),
        pl.BlockSpec((tm, GLA_KEY_WIDTH), lambda i: (i, 0)),
        pl.BlockSpec((1, FOX_HEADS, tm), c_map),
    ]
    out_shape = [
        jax.ShapeDtypeStruct((m, D_MODEL), BF16),
        jax.ShapeDtypeStruct(c_shape, F32),
        jax.ShapeDtypeStruct((m, GLA_KEY_WIDTH), F32),
        jax.ShapeDtypeStruct(c_shape, F32),
    ]
    if emit_cx:
        out_specs.append(pl.BlockSpec((tm, LANES), lambda i: (i, 0)))
        out_shape.append(jax.ShapeDtypeStruct((m, LANES), BF16))
    return pl.pallas_call(
        functools.partial(_norm_small_kernel, tm=tm, seg=seg, emit_cx=emit_cx),
        grid=(n_tiles,),
        in_specs=[
            pl.BlockSpec((tm, D_MODEL), lambda i: (i, 0)),
            pl.BlockSpec((1, 1, D_MODEL), lay),
            pl.BlockSpec((1, LANES, D_MODEL), lay),
            pl.BlockSpec((1, LANES, GLA_KEY_WIDTH), lay),
            pl.BlockSpec((1, 1, GLA_KEY_WIDTH), lay),
            pl.BlockSpec((1, FOX_HEADS, 1), lay),
        ],
        out_specs=out_specs,
        out_shape=out_shape,
        scratch_shapes=[pltpu.VMEM((FOX_HEADS, LANES), F32)],
        compiler_params=_params(("arbitrary",)),
        name="norm_small",
    )(x, gamma, w_small, wa2_pad, b_a, bf_t)


def _mm_kernel(*refs, n_a, w_t, has_scale, has_res, has_stack_in, n_out):
    a_refs = refs[:n_a]
    w_refs = refs[n_a:2 * n_a]
    pos = 2 * n_a
    mm = _dot_nt if w_t else _dot
    acc = mm(a_refs[0][...], w_refs[0][0].astype(BF16))
    for a_ref, w_ref in zip(a_refs[1:], w_refs[1:]):
        acc = acc + mm(a_ref[...], w_ref[0].astype(BF16))
    if has_scale:
        acc = acc * refs[pos][...]
        pos += 1
    if has_res:
        acc = refs[pos][...] + acc
        pos += 1
    if has_stack_in:
        pos += 1
    for o_ref in refs[pos:pos + n_out]:
        o_ref[...] = acc.astype(o_ref.dtype)


def matmul(a_list, w_list, *, layer, out_dtypes, tm, tn, n, w_rows=None, w_col0=0, w_t=False,
           scale=None, res=None, stacked=None):
    m = a_list[0].shape[0]
    n_i = m // tm
    w_rows = w_rows or [0] * len(a_list)
    in_specs = [pl.BlockSpec((tm, a.shape[1]), lambda i, j: (i, 0)) for a in a_list]
    for a, w, r in zip(a_list, w_list, w_rows):
        if w_t:
            in_specs.append(pl.BlockSpec((1, tn, a.shape[1]), lambda i, j, r=r: (layer, w_col0 + j, r)))
        else:
            in_specs.append(pl.BlockSpec((1, a.shape[1], tn), lambda i, j, r=r: (layer, r, w_col0 + j)))
    args = list(a_list) + list(w_list)
    if scale is not None:
        in_specs.append(pl.BlockSpec((1, tn), lambda i, j: (0, j)))
        args.append(scale)
    if res is not None:
        in_specs.append(pl.BlockSpec((tm, tn), lambda i, j: (i, j)))
        args.append(res)
    out_specs = [pl.BlockSpec((tm, tn), lambda i, j: (i, j)) for _ in out_dtypes]
    out_shape = [jax.ShapeDtypeStruct((m, n), dt) for dt in out_dtypes]
    aliases = {}
    has_stack_in = False
    if stacked is not None:
        out_specs[0] = pl.BlockSpec((tm, tn), lambda i, j: (layer * n_i + i, j))
        out_shape[0] = jax.ShapeDtypeStruct((DEPTH * m, n), out_dtypes[0])
        if stacked[0] is not None:
            has_stack_in = True
            in_specs.append(pl.BlockSpec(memory_space=pl.ANY))
            aliases = {len(args): 0}
            args.append(stacked[0])
    outs = pl.pallas_call(
        functools.partial(_mm_kernel, n_a=len(a_list), w_t=w_t, has_scale=scale is not None,
                          has_res=res is not None, has_stack_in=has_stack_in, n_out=len(out_dtypes)),
        grid=(n_i, n // tn),
        in_specs=in_specs,
        out_specs=out_specs,
        out_shape=out_shape,
        input_output_aliases=aliases,
        compiler_params=_params(("parallel", "arbitrary")),
        name="matmul",
    )(*args)
    return outs


def _fox_prompt_kernel(q_ref, k_ref, cx_ref, v_ref, gn_ref, o_ref, kx_sc, vt_sc, *, seq, tq):
    h = pl.program_id(1)
    kx_sc[:, 0:FOX_DIM] = k_ref[...]
    kx_sc[:, FOX_DIM:2 * FOX_DIM] = cx_ref[...]
    vt_sc[0:FOX_DIM, :] = v_ref[...].astype(F32).T.astype(BF16)
    vt_sc[FOX_DIM:FOX_DIM + BF16_ROWS, :] = jnp.ones((BF16_ROWS, seq), BF16)

    lane = lax.broadcasted_iota(jnp.int32, (tq, FOX_DIM), 1)
    pick = (lane == h) | (lane == h + FOX_HEADS) | (lane == h + 2 * FOX_HEADS)
    onehot = jnp.where(pick, 1.0, 0.0).astype(BF16)
    key_i = lax.broadcasted_iota(jnp.int32, (tq, tq), 0)
    qry_i = lax.broadcasted_iota(jnp.int32, (tq, tq), 1)
    causal = key_i <= qry_i

    def scores(qi):
        lo, hi = qi * tq, (qi + 1) * tq
        qx = jnp.concatenate([q_ref[lo:hi, :], onehot], axis=1)
        s_diag = jnp.where(causal, _dot_nt(kx_sc[lo:hi, :], qx), NEG_BIG)
        s_off = _dot_nt(kx_sc[0:lo, :], qx) if qi > 0 else None
        return s_diag, s_off

    n_q = seq // tq
    ahead = scores(0)
    for qi in range(n_q):
        lo, hi = qi * tq, (qi + 1) * tq
        s_diag, s_off = ahead
        if qi + 1 < n_q:
            ahead = scores(qi + 1)
        m = jnp.max(s_diag, axis=0, keepdims=True)
        if qi > 0:
            m = jnp.maximum(m, jnp.max(s_off, axis=0, keepdims=True))
            p_off = jnp.exp2(s_off - m).astype(BF16)
            ot = _dot(vt_sc[:, 0:lo], p_off)
        p_diag = jnp.exp2(s_diag - m).astype(BF16)
        ot_d = _dot(vt_sc[:, lo:hi], p_diag)
        ot = ot + ot_d if qi > 0 else ot_d
        o = ot[0:FOX_DIM, :] * (1.0 / ot[FOX_DIM:FOX_DIM + 1, :])
        y = o * lax.rsqrt(jnp.mean(o * o, axis=0, keepdims=True) + EPS)
        o_ref[lo:hi, :] = (y.T * gn_ref[0]).astype(o_ref.dtype)


def fox_prompt(q, k16, cx, v16, fox_norm, *, layer, batch, seq):
    m = batch * seq
    return pl.pallas_call(
        functools.partial(_fox_prompt_kernel, seq=seq, tq=TQ),
        grid=(batch, FOX_HEADS),
        in_specs=[
            pl.BlockSpec((seq, FOX_DIM), lambda b, h: (b, h)),
            pl.BlockSpec((seq, FOX_DIM), lambda b, h: (b, h)),
            pl.BlockSpec((seq, LANES), lambda b, h: (b, 0)),
            pl.BlockSpec((seq, FOX_DIM), lambda b, h: (b, h)),
            pl.BlockSpec((1, 1, FOX_DIM), lambda b, h: (layer, 0, h)),
        ],
        out_specs=pl.BlockSpec((seq, FOX_DIM), lambda b, h: (b, h)),
        out_shape=jax.ShapeDtypeStruct((m, FOX_WIDTH), BF16),
        scratch_shapes=[pltpu.VMEM((seq, 2 * FOX_DIM), BF16),
                        pltpu.VMEM((FOX_DIM + BF16_ROWS, seq), BF16)],
        compiler_params=_params(("parallel", "arbitrary")),
        name="fox_prompt",
    )(q, k16, cx, v16, fox_norm)


def _fox_sample_kernel(pt_ref, q_ref, kn_ref, vn_ref, cn_ref, gn_ref, *rest, pages, t_new):
    k_refs = rest[:pages]
    v_refs = rest[pages:2 * pages]
    lf_refs = rest[2 * pages:3 * pages]
    o_ref, m_sc, l_sc, acc_sc, carry_sc = rest[3 * pages:]
    del pt_ref
    step = pl.program_id(1)
    n_steps = pl.num_programs(1)
    rows = t_new

    @pl.when(step == 0)
    def _():
        m_sc[...] = jnp.full_like(m_sc, NEG_BIG)
        l_sc[...] = jnp.zeros_like(l_sc)
        acc_sc[...] = jnp.zeros_like(acc_sc)
        carry_sc[...] = jnp.zeros_like(carry_sc)

    flat = PAGE * FOX_HEADS
    lane = lax.broadcasted_iota(jnp.int32, (pages, flat), 1)
    lf = jnp.concatenate([lf_refs[p][0, 0] for p in range(pages)], axis=0)
    suf, tot = lf, lf
    d = FOX_HEADS
    while d < flat:
        suf = suf + jnp.where(lane < flat - d, pltpu.roll(suf, flat - d, axis=1), 0.0)
        tot = tot + pltpu.roll(tot, d, axis=1)
        d *= 2
    run = carry_sc[0:1, :]
    bias = [None] * pages
    for p in range(pages - 1, -1, -1):
        bias[p] = ((suf[p:p + 1, :] - lf[p:p + 1, :]) + run) * LOG2E
        run = run + tot[p:p + 1, :]
    carry_sc[0:1, :] = run

    n_rows = FOX_HEADS * rows
    q_flat = jnp.concatenate(
        [q_ref[:, h * FOX_DIM:(h + 1) * FOX_DIM] for h in range(FOX_HEADS)], axis=0).astype(BF16)
    zeros = jnp.zeros_like(q_flat)
    q_pair = jnp.concatenate([jnp.concatenate([q_flat, zeros], axis=1),
                              jnp.concatenate([zeros, q_flat], axis=1)], axis=0)
    row_i = lax.broadcasted_iota(jnp.int32, (n_rows, flat), 0)
    col_i = lax.broadcasted_iota(jnp.int32, (n_rows, flat), 1)
    own = (row_i >> (rows.bit_length() - 1)) == (col_i & (FOX_HEADS - 1))

    def local_softmax(s, v_all):
        m = jnp.max(s, axis=-1, keepdims=True)
        pr = jnp.exp2(s - m)
        return m, jnp.sum(pr, axis=-1, keepdims=True), _dot(pr.astype(BF16), v_all)

    def merge(groups):
        m_old = m_sc[...]
        m_new = m_old
        for m, _, _ in groups:
            m_new = jnp.maximum(m_new, m)
        alpha = jnp.exp2(m_old - m_new)
        l_new = alpha * l_sc[...]
        acc = alpha * acc_sc[...]
        for m, l, o in groups:
            w = jnp.exp2(m - m_new)
            l_new = l_new + w * l
            acc = acc + w * o
        m_sc[...] = m_new
        l_sc[...] = l_new
        acc_sc[...] = acc

    s_parts = [None] * pages
    for a in range(0, pages, 2):
        kk = jnp.concatenate([k_refs[a][0, 0], k_refs[a + 1][0, 0]], axis=1).astype(BF16)
        s2 = _dot_nt(q_pair, kk)
        s_parts[a] = jnp.where(own, s2[0:n_rows] + bias[a], NEG_BIG)
        s_parts[a + 1] = jnp.where(own, s2[n_rows:2 * n_rows] + bias[a + 1], NEG_BIG)
    half = pages // 2
    groups = []
    for lo_p, hi_p in ((0, half), (half, pages)):
        groups.append(local_softmax(
            jnp.concatenate(s_parts[lo_p:hi_p], axis=1),
            jnp.concatenate([v_refs[p][0, 0].astype(BF16) for p in range(lo_p, hi_p)], axis=0)))
    merge(groups)

    @pl.when(step == n_steps - 1)
    def _():
        valid = own & ((col_i >> (FOX_HEADS.bit_length() - 1)) <= (row_i & (rows - 1)))
        s_new = _dot_nt(q_flat, kn_ref[...]) - cn_ref[0] * LOG2E
        merge([local_softmax(jnp.where(valid, s_new, NEG_BIG), vn_ref[...])])
        for h in range(FOX_HEADS):
            cols = slice(h * FOX_DIM, (h + 1) * FOX_DIM)
            sl = slice(h * rows, (h + 1) * rows)
            o = acc_sc[sl, :] / l_sc[sl, :]
            o_ref[:, cols] = _rms(o, gn_ref[0, :, cols])


def fox_sample(page_table, q, k_new_page, v_new_page, c_new_flat, fox_norm, cache_k4, cache_v4,
               cache_lf_flat, *, layer, pages):
    db, n_pages = page_table.shape
    t_new = q.shape[0] // db
    assert t_new == SUBLANES and pages % 2 == 0 and n_pages % pages == 0
    n_steps = n_pages // pages
    flat = PAGE * FOX_HEADS
    pt_flat = page_table.reshape(-1)

    def page_map(p):
        def index(b, s, pt):
            return (layer, pt[b * n_pages + (n_steps - 1 - s) * pages + p], 0, 0)
        return index

    in_specs = [
        pl.BlockSpec((t_new, FOX_WIDTH), lambda b, s, pt: (b, 0)),
        pl.BlockSpec((flat, FOX_DIM), lambda b, s, pt: (b, 0)),
        pl.BlockSpec((flat, FOX_DIM), lambda b, s, pt: (b, 0)),
        pl.BlockSpec((1, 1, flat), lambda b, s, pt: (b, 0, 0)),
        pl.BlockSpec((1, 1, FOX_WIDTH), lambda b, s, pt: (layer, 0, 0)),
    ]
    in_specs += [pl.BlockSpec((1, 1, flat, FOX_DIM), page_map(p)) for p in range(pages)]
    in_specs += [pl.BlockSpec((1, 1, flat, FOX_DIM), page_map(p)) for p in range(pages)]
    in_specs += [pl.BlockSpec((1, 1, 1, flat), page_map(p)) for p in range(pages)]
    grid_spec = pltpu.PrefetchScalarGridSpec(
        num_scalar_prefetch=1,
        grid=(db, n_steps),
        in_specs=in_specs,
        out_specs=pl.BlockSpec((t_new, FOX_WIDTH), lambda b, s, pt: (b, 0)),
        scratch_shapes=[pltpu.VMEM((FOX_HEADS * t_new, 1), F32),
                        pltpu.VMEM((FOX_HEADS * t_new, 1), F32),
                        pltpu.VMEM((FOX_HEADS * t_new, FOX_DIM), F32),
                        pltpu.VMEM((SUBLANES, flat), F32)],
    )
    return pl.pallas_call(
        functools.partial(_fox_sample_kernel, pages=pages, t_new=t_new),
        grid_spec=grid_spec,
        out_shape=jax.ShapeDtypeStruct((db * t_new, FOX_WIDTH), F32),
        compiler_params=_params(("parallel", "arbitrary")),
        name="fox_sample",
    )(pt_flat, q, k_new_page, v_new_page, c_new_flat, fox_norm,
      *([cache_k4] * pages), *([cache_v4] * pages), *([cache_lf_flat] * pages))


def _gla_kernel(q_ref, k_ref, v_ref, la_ref, r_ref, gn_ref, s0_ref, o_ref, sfin_ref,
                s_sc, b_sc, *, chunk, sub, hb):
    ci = pl.program_id(2)
    n_chunks = pl.num_programs(2)
    heads = range(hb)
    kcols = lambda h: slice(h * GLA_DK, (h + 1) * GLA_DK)
    vcols = lambda h: slice(h * GLA_DV, (h + 1) * GLA_DV)

    @pl.when(ci == 0)
    def _():
        s_sc[...] = s0_ref[0]

    q = [q_ref[:, kcols(h)].astype(F32) for h in heads]
    k = [k_ref[:, kcols(h)].astype(F32) for h in heads]
    v = [v_ref[:, vcols(h)] for h in heads]
    rowi = lax.broadcasted_iota(jnp.int32, (chunk, chunk), 0)
    coli = lax.broadcasted_iota(jnp.int32, (chunk, chunk), 1)

    tri = jnp.where(rowi >= coli, 1.0, 0.0).astype(BF16)
    hi, mid, lo = _split3(la_ref[...])
    b_all = _dot(tri, hi.astype(BF16)) + _dot(tri, mid.astype(BF16)) + _dot(tri, lo.astype(BF16))
    b_sc[...] = b_all
    b = [b_all[:, kcols(h)] for h in heads]
    b_last = [b_sc[chunk - 1:chunk, kcols(h)] for h in heads]

    ones = jnp.ones((GLA_DK, chunk), BF16)
    diff = rowi - coli
    sub_shift = sub.bit_length() - 1
    a = [jnp.zeros((chunk, chunk), F32) for _ in heads]
    for d in range(sub):
        for h in heads:
            k_s = k[h] if d == 0 else pltpu.roll(k[h], d, axis=0)
            b_s = b[h] if d == 0 else pltpu.roll(b[h], d, axis=0)
            prod = q[h] * k_s * jnp.exp(b[h] - b_s)
            band = _dot(prod.astype(BF16), ones)
            a[h] = jnp.where(diff == d, band, a[h])
    in_sub = (rowi >> sub_shift) == (coli >> sub_shift)
    a = [jnp.where(in_sub, a[h], 0.0) for h in heads]

    m = sub
    while m < chunk:
        groups = chunk // (2 * m)
        shift = (2 * m).bit_length() - 1
        keep = ((rowi >> shift) == (coli >> shift)) & ((rowi & m) != 0) & ((coli & m) == 0)
        for h in heads:
            ref_rows = [jnp.broadcast_to(b_sc[g * 2 * m + m - 1:g * 2 * m + m, kcols(h)], (2 * m, GLA_DK))
                        for g in range(groups)]
            ref_b = ref_rows[0] if groups == 1 else jnp.concatenate(ref_rows, axis=0)
            e = jnp.exp(-jnp.abs(b[h] - ref_b))
            qm = (q[h] * e).astype(BF16)
            km = (k[h] * e).astype(BF16)
            a[h] = a[h] + jnp.where(keep, _dot_nt(qm, km), 0.0)
        m *= 2

    s_old = [s_sc[h] for h in heads]
    o = [_dot((q[h] * jnp.exp(b[h])).astype(BF16), s_old[h].astype(BF16))
         + _dot(a[h].astype(BF16), v[h]) for h in heads]

    s_new = []
    for h in heads:
        kl = (k[h] * jnp.exp(b_last[h] - b[h])).astype(BF16)
        decay = jnp.broadcast_to(jnp.exp(b_last[h]), (GLA_DK, GLA_DK)).T
        decay = jnp.concatenate([decay, decay], axis=1)
        s_new.append(decay * s_old[h] + _dot_tn(kl, v[h]))
        s_sc[h] = s_new[h]

    @pl.when(ci == n_chunks - 1)
    def _():
        for h in heads:
            sfin_ref[0, h] = s_new[h]

    for h in heads:
        r = r_ref[:, vcols(h)].astype(F32)
        o_ref[:, vcols(h)] = (_rms(o[h], gn_ref[0, :, vcols(h)]) * _silu(r)).astype(o_ref.dtype)


def gla(gp, la, gla_norm, s0, *, layer, batch, seq):
    m = batch * seq
    chunk = GLA_CHUNK
    hb = GLA_HEAD_BLOCK
    kw, vw = hb * GLA_DK, hb * GLA_DV
    n_chunks = seq // chunk
    rows = lambda b, h, c: b * n_chunks + c
    return pl.pallas_call(
        functools.partial(_gla_kernel, chunk=chunk, sub=GLA_SUB, hb=hb),
        grid=(batch, GLA_HEADS // hb, n_chunks),
        in_specs=[
            pl.BlockSpec((chunk, kw), lambda b, h, c: (rows(b, h, c), G_GQ // kw + h)),
            pl.BlockSpec((chunk, kw), lambda b, h, c: (rows(b, h, c), G_GK // kw + h)),
            pl.BlockSpec((chunk, vw), lambda b, h, c: (rows(b, h, c), G_GV // vw + h)),
            pl.BlockSpec((chunk, kw), lambda b, h, c: (rows(b, h, c), h)),
            pl.BlockSpec((chunk, vw), lambda b, h, c: (rows(b, h, c), G_GR // vw + h)),
            pl.BlockSpec((1, 1, vw), lambda b, h, c: (layer, 0, h)),
            pl.BlockSpec((1, hb, GLA_DK, GLA_DV), lambda b, h, c: (b, h, 0, 0)),
        ],
        out_specs=[
            pl.BlockSpec((chunk, vw), lambda b, h, c: (rows(b, h, c), h)),
            pl.BlockSpec((1, hb, GLA_DK, GLA_DV), lambda b, h, c: (b, h, 0, 0)),
        ],
        out_shape=[
            jax.ShapeDtypeStruct((m, GLA_WIDTH), BF16),
            jax.ShapeDtypeStruct((batch, GLA_HEADS, GLA_DK, GLA_DV), F32),
        ],
        scratch_shapes=[pltpu.VMEM((hb, GLA_DK, GLA_DV), F32), pltpu.VMEM((chunk, kw), F32)],
        compiler_params=_params(("parallel", "parallel", "arbitrary")),
        name="gla",
    )(gp, gp, gp, la, gp, gla_norm, s0)


def _conv_gate(g, g1, g2, u, cw_ref, cb_ref, cols):
    conv = (cb_ref[0, :, cols] + cw_ref[0, 0:1, cols] * g2 + cw_ref[0, 1:2, cols] * g1
            + cw_ref[0, 2:3, cols] * g)
    return _silu(conv) * u


def _ffn_up_prompt_kernel(x_ref, halo_ref, g_ref, wg_ref, wu_ref, cw_ref, cb_ref,
                          h_ref, gtail_ref, xn_sc, *, tm, tn, tiles_per_seq):
    i = pl.program_id(0)
    j = pl.program_id(1)

    @pl.when(j == 0)
    def _():
        xn_sc[0:tm, :] = _rms(x_ref[...], g_ref[0]).astype(BF16)
        keep = jnp.where(i % tiles_per_seq == 0, 0.0, 1.0)
        xn_sc[tm:tm + BF16_ROWS, :] = (_rms(halo_ref[...], g_ref[0]) * keep).astype(BF16)

    for t in range(tn // MXU_COLS):
        cols = slice(t * MXU_COLS, (t + 1) * MXU_COLS)
        g_ext = _dot(xn_sc[...], wg_ref[0, :, cols].astype(BF16))
        u = _dot(xn_sc[0:tm, :], wu_ref[0, :, cols].astype(BF16))
        g = g_ext[0:tm]
        g1 = pltpu.roll(g_ext, 1, axis=0)[0:tm]
        g2 = pltpu.roll(g_ext, 2, axis=0)[0:tm]
        h_ref[:, cols] = _conv_gate(g, g1, g2, u, cw_ref, cb_ref, cols).astype(h_ref.dtype)
        gtail_ref[:, cols] = g[tm - SUBLANES:tm]


def ffn_up_prompt(x, gamma, wg, wu, conv_w, conv_b, *, layer, seq, tm, tn):
    m = x.shape[0]
    tiles_per_seq = seq // tm
    halo_blocks = tm // BF16_ROWS
    return pl.pallas_call(
        functools.partial(_ffn_up_prompt_kernel, tm=tm, tn=tn, tiles_per_seq=tiles_per_seq),
        grid=(m // tm, D_FF // tn),
        in_specs=[
            pl.BlockSpec((tm, D_MODEL), lambda i, j: (i, 0)),
            pl.BlockSpec((BF16_ROWS, D_MODEL), lambda i, j: (jnp.maximum(i * halo_blocks - 1, 0), 0)),
            pl.BlockSpec((1, 1, D_MODEL), lambda i, j: (layer, 0, 0)),
            pl.BlockSpec((1, D_MODEL, tn), lambda i, j: (layer, 0, j)),
            pl.BlockSpec((1, D_MODEL, tn), lambda i, j: (layer, 0, j)),
            pl.BlockSpec((1, CONV_W, tn), lambda i, j: (layer, 0, j)),
            pl.BlockSpec((1, 1, tn), lambda i, j: (layer, 0, j)),
        ],
        out_specs=[
            pl.BlockSpec((tm, tn), lambda i, j: (i, j)),
            pl.BlockSpec((SUBLANES, tn), lambda i, j: (i, j)),
        ],
        out_shape=[
            jax.ShapeDtypeStruct((m, D_FF), BF16),
            jax.ShapeDtypeStruct((m // tm * SUBLANES, D_FF), F32),
        ],
        scratch_shapes=[pltpu.VMEM((tm + BF16_ROWS, D_MODEL), BF16)],
        compiler_params=_params(("parallel", "arbitrary")),
        name="ffn_up_prompt",
    )(x, x, gamma, wg, wu, conv_w, conv_b)


def _ffn_up_sample_kernel(x_ref, g_ref, wg_ref, wu_ref, cw_ref, cb_ref, e1_ref, e2_ref,
                          h_ref, gout_ref, xn_sc, *, seq, tn):
    j = pl.program_id(0)

    @pl.when(j == 0)
    def _():
        xn_sc[...] = _rms(x_ref[...], g_ref[0]).astype(BF16)

    cols = slice(0, tn)
    g = _dot(xn_sc[...], wg_ref[0].astype(BF16))
    u = _dot(xn_sc[...], wu_ref[0].astype(BF16))
    pos = lax.broadcasted_iota(jnp.int32, (g.shape[0], 1), 0) & (seq - 1)
    g1 = jnp.where(pos >= 1, pltpu.roll(g, 1, axis=0), e1_ref[...])
    g2 = jnp.where(pos >= 2, pltpu.roll(g, 2, axis=0), e2_ref[...])
    h_ref[...] = _conv_gate(g, g1, g2, u, cw_ref, cb_ref, cols).astype(h_ref.dtype)
    gout_ref[...] = g


def ffn_up_sample(x, gamma, wg, wu, conv_w, conv_b, e1, e2, *, layer, seq, tn):
    m = x.shape[0]
    return pl.pallas_call(
        functools.partial(_ffn_up_sample_kernel, seq=seq, tn=tn),
        grid=(D_FF // tn,),
        in_specs=[
            pl.BlockSpec((m, D_MODEL), lambda j: (0, 0)),
            pl.BlockSpec((1, 1, D_MODEL), lambda j: (layer, 0, 0)),
            pl.BlockSpec((1, D_MODEL, tn), lambda j: (layer, 0, j)),
            pl.BlockSpec((1, D_MODEL, tn), lambda j: (layer, 0, j)),
            pl.BlockSpec((1, CONV_W, tn), lambda j: (layer, 0, j)),
            pl.BlockSpec((1, 1, tn), lambda j: (layer, 0, j)),
            pl.BlockSpec((m, tn), lambda j: (0, j)),
            pl.BlockSpec((m, tn), lambda j: (0, j)),
        ],
        out_specs=[
            pl.BlockSpec((m, tn), lambda j: (0, j)),
            pl.BlockSpec((m, tn), lambda j: (0, j)),
        ],
        out_shape=[
            jax.ShapeDtypeStruct((m, D_FF), BF16),
            jax.ShapeDtypeStruct((m, D_FF), F32),
        ],
        scratch_shapes=[pltpu.VMEM((m, D_MODEL), BF16)],
        compiler_params=_params(("arbitrary",)),
        name="ffn_up_sample",
    )(x, gamma, wg, wu, conv_w, conv_b, e1, e2)


def _rmsnorm_kernel(x_ref, g_ref, o_ref):
    o_ref[...] = _rms(x_ref[...], g_ref[...])


def rmsnorm(x, gamma, *, tm):
    m = x.shape[0]
    return pl.pallas_call(
        _rmsnorm_kernel,
        grid=(m // tm,),
        in_specs=[pl.BlockSpec((tm, D_MODEL), lambda i: (i, 0)),
                  pl.BlockSpec((1, D_MODEL), lambda i: (0, 0))],
        out_specs=pl.BlockSpec((tm, D_MODEL), lambda i: (i, 0)),
        out_shape=jax.ShapeDtypeStruct((m, D_MODEL), F32),
        compiler_params=_params(("parallel",)),
        name="rmsnorm",
    )(x, gamma)


def _window_cast_kernel(a_ref, b_ref, o_ref, *, shift):
    x = jnp.concatenate([a_ref[0], b_ref[0]], axis=0)
    o_ref[0] = x[shift:, :].astype(o_ref.dtype)


def window_cast(w_t, *, row0, width, rows):
    depth, _, k_dim = w_t.shape
    blk, shift = divmod(row0, rows)
    assert shift == SUBLANES and width % rows == 0
    return pl.pallas_call(
        functools.partial(_window_cast_kernel, shift=shift),
        grid=(depth, width // rows),
        in_specs=[pl.BlockSpec((1, rows, k_dim), lambda l, i: (l, blk + i, 0)),
                  pl.BlockSpec((1, shift, k_dim), lambda l, i: (l, (blk + i + 1) * (rows // shift), 0))],
        out_specs=pl.BlockSpec((1, rows, k_dim), lambda l, i: (l, i, 0)),
        out_shape=jax.ShapeDtypeStruct((depth, width, k_dim), BF16),
        compiler_params=_params(("parallel", "parallel")),
        name="window_cast",
    )(w_t, w_t)


def _prep_weights(w_in_t, b_f, w_a2, b_a, w_out, w_gate, w_up, w_down):
    o_fl = 3 * FOX_WIDTH
    o_gq = o_fl + FOX_HEADS
    o_ga = o_gq + 2 * GLA_KEY_WIDTH + 2 * GLA_WIDTH
    w_fl, w_ga = lax.optimization_barrier(
        (w_in_t[:, o_fl:o_fl + FOX_HEADS, :], w_in_t[:, o_ga:o_ga + GLA_RANK, :]))
    pad = LANES - FOX_HEADS - GLA_RANK
    zeros = lambda *s: jnp.zeros((DEPTH,) + s, F32)
    return dict(
        w_g=window_cast(w_in_t, row0=o_gq, width=G_WIDTH, rows=256),
        w_small=jnp.concatenate([w_fl, w_ga, zeros(pad, D_MODEL)], axis=1).astype(BF16),
        wa2_pad=jnp.concatenate([zeros(FOX_HEADS, GLA_KEY_WIDTH), w_a2, zeros(pad, GLA_KEY_WIDTH)],
                                axis=1).astype(BF16),
        b_a=b_a.reshape(DEPTH, 1, GLA_KEY_WIDTH),
        bf_t=b_f.reshape(DEPTH, FOX_HEADS, 1),
        w_out=w_out.astype(BF16), w_gate=w_gate, w_up=w_up,
        w_down=w_down.astype(BF16),
    )


def _pad_rows(x, batch, t, t_pad):
    w = x.shape[-1]
    return jnp.pad(x.reshape(batch, t, w), ((0, 0), (0, t_pad - t), (0, 0))).reshape(batch * t_pad, w)


def kernel(x_prompt, x_sample, cache_k, cache_v, cache_logf, state_gla, state_conv, page_table,
           w_in, b_f, w_a2, b_a, fox_norm, gla_norm, w_out, norm_attn, norm_ffn,
           w_gate, w_up, conv_w, conv_b, w_down, norm_final):
    bp, seq, _ = x_prompt.shape
    db, t_new, _ = x_sample.shape
    n_pool = cache_k.shape[1]
    mp, ms = bp * seq, db * t_new

    xp = x_prompt.reshape(mp, D_MODEL)
    xs = x_sample.reshape(ms, D_MODEL)
    cache_k4 = cache_k.reshape(DEPTH, n_pool, PAGE * FOX_HEADS, FOX_DIM)
    cache_v4 = cache_v.reshape(DEPTH, n_pool, PAGE * FOX_HEADS, FOX_DIM)
    cache_lf_flat = cache_logf.reshape(DEPTH, n_pool, 1, PAGE * FOX_HEADS)
    q_scale = jnp.full((1, FOX_WIDTH), FOX_DIM ** -0.5 * LOG2E, F32)
    g_scale = jnp.concatenate([jnp.full((GLA_KEY_WIDTH,), GLA_DK ** -0.5, F32),
                               jnp.ones((G_WIDTH - GLA_KEY_WIDTH,), F32)]).reshape(1, G_WIDTH)
    zero_state = jnp.zeros((bp, GLA_HEADS, GLA_DK, GLA_DV), F32)
    w_in_t = jnp.swapaxes(w_in, 1, 2)
    w = _prep_weights(w_in_t, b_f, w_a2, b_a, w_out, w_gate, w_up, w_down)
    conv_b3 = conv_b.reshape(DEPTH, 1, D_FF)
    norm_attn = norm_attn.reshape(DEPTH, 1, D_MODEL)
    norm_ffn = norm_ffn.reshape(DEPTH, 1, D_MODEL)
    fox_norm = fox_norm.reshape(DEPTH, 1, FOX_WIDTH)
    gla_norm = gla_norm.reshape(DEPTH, 1, GLA_WIDTH)
    small = (w["w_small"], w["wa2_pad"], w["b_a"], w["bf_t"])

    def projections(xn, l, tm, k_stack, v_stack):
        wide = 2 * TN
        (q,) = matmul([xn], [w_in_t], layer=l, out_dtypes=[BF16], tm=tm, tn=wide, n=FOX_WIDTH, w_col0=0,
                      w_t=True, scale=q_scale)
        k_stack, k16 = matmul([xn], [w_in_t], layer=l, out_dtypes=[F32, BF16], tm=tm, tn=wide, n=FOX_WIDTH,
                              w_col0=1, w_t=True, stacked=(k_stack,))
        v_stack, v16 = matmul([xn], [w_in_t], layer=l, out_dtypes=[F32, BF16], tm=tm, tn=wide, n=FOX_WIDTH,
                              w_col0=2, w_t=True, stacked=(v_stack,))
        (gp,) = matmul([xn], [w["w_g"]], layer=l, out_dtypes=[BF16], tm=tm, tn=wide, n=G_WIDTH, w_t=True,
                       scale=g_scale)
        return q, k_stack, k16, v_stack, v16, gp

    def out_and_down(fo, go, h_fn, x, l, tm_out, tm):
        (x,) = matmul([fo, go], [w["w_out"], w["w_out"]], layer=l, out_dtypes=[F32], tm=tm_out, tn=D_MODEL,
                      n=D_MODEL, w_rows=[0, 1], res=x)
        h, extra = h_fn(x)
        (x,) = matmul([h], [w["w_down"]], layer=l, out_dtypes=[F32], tm=tm, tn=TN, n=D_MODEL, res=x)
        return x, extra

    outs = {k: [] for k in ("lp", "sp", "cp", "ls", "ss", "cs")}
    kp = vp = ksm = vsm = None
    for l in range(DEPTH):
        xn, logf, la, _, cx = norm_small(xp, norm_attn, *small, layer=l, tm=TM, seg=seq, emit_cx=True)
        q, kp, k16, vp, v16, gp = projections(xn, l, TM, kp, vp)
        fo = fox_prompt(q, k16, cx, v16, fox_norm, layer=l, batch=bp, seq=seq)
        go, s_fin = gla(gp, la, gla_norm, zero_state, layer=l, batch=bp, seq=seq)
        ffn = lambda x: ffn_up_prompt(x, norm_ffn, w["w_gate"], w["w_up"], conv_w, conv_b3,
                                      layer=l, seq=seq, tm=TM, tn=TN)
        xp, gtail = out_and_down(fo, go, ffn, xp, l, TM // 2, TM)
        outs["lp"].append(logf.transpose(0, 2, 1))
        outs["sp"].append(s_fin)
        outs["cp"].append(gtail.reshape(bp, seq // TM, SUBLANES, D_FF)[:, -1, SUBLANES - (CONV_W - 1):, :])

        xn, logf, la, c = norm_small(xs, norm_attn, *small, layer=l, tm=ms, seg=t_new, emit_cx=False)
        q, ksm, k16, vsm, v16, gp = projections(xn, l, ms, ksm, vsm)
        kn_page = _pad_rows(k16, db, t_new, PAGE).reshape(db * PAGE * FOX_HEADS, FOX_DIM)
        vn_page = _pad_rows(v16, db, t_new, PAGE).reshape(db * PAGE * FOX_HEADS, FOX_DIM)
        c_new = c.reshape(FOX_HEADS, db, t_new).transpose(1, 2, 0)
        c_new_flat = jnp.pad(c_new, ((0, 0), (0, PAGE - t_new), (0, 0))).reshape(db, 1, PAGE * FOX_HEADS)
        fo = fox_sample(page_table, q.astype(F32), kn_page, vn_page, c_new_flat, fox_norm,
                        cache_k4, cache_v4, cache_lf_flat, layer=l, pages=SAMPLE_PAGES)
        gp_pad = _pad_rows(gp, db, t_new, GLA_CHUNK)
        la_pad = _pad_rows(la, db, t_new, GLA_CHUNK)
        go_pad, s_fin = gla(gp_pad, la_pad, gla_norm, state_gla[l], layer=l, batch=db, seq=GLA_CHUNK)
        go = go_pad.reshape(db, GLA_CHUNK, GLA_WIDTH)[:, :t_new].reshape(ms, GLA_WIDTH)
        st = state_conv[l]
        zeros_f = jnp.zeros((db, t_new - 1, D_FF), F32)
        e1 = jnp.concatenate([st[:, 1:2], zeros_f], axis=1).reshape(ms, D_FF)
        e2 = jnp.concatenate([st[:, 0:2], zeros_f[:, 1:]], axis=1).reshape(ms, D_FF)
        ffn = lambda x: ffn_up_sample(x, norm_ffn, w["w_gate"], w["w_up"], conv_w, conv_b3, e1, e2,
                                      layer=l, seq=t_new, tn=TN)
        xs, g_all = out_and_down(fo.astype(BF16), go, ffn, xs, l, ms, ms)
        outs["ls"].append(logf.reshape(FOX_HEADS, db, t_new).transpose(1, 2, 0))
        outs["ss"].append(s_fin)
        outs["cs"].append(g_all.reshape(db, t_new, D_FF)[:, t_new - (CONV_W - 1):, :])

    g_fin = norm_final.reshape(1, D_MODEL)
    y_prompt = rmsnorm(xp, g_fin, tm=TM).reshape(bp, seq, D_MODEL)
    y_sample = rmsnorm(xs, g_fin, tm=ms).reshape(db, t_new, D_MODEL)
    stack = lambda key: jnp.stack(outs[key])
    kv_p = lambda t: t.reshape(DEPTH, bp, seq, FOX_HEADS, FOX_DIM)
    kv_s = lambda t: t.reshape(DEPTH, db, t_new, FOX_HEADS, FOX_DIM)
    return (y_prompt, y_sample,
            kv_p(kp), kv_p(vp), stack("lp"), stack("sp"), stack("cp"),
            kv_s(ksm), kv_s(vsm), stack("ls"), stack("ss"), stack("cs"))
```

```python
import functools
import math

import jax
import jax.numpy as jnp
from jax import lax
from jax.experimental import pallas as pl
from jax.experimental.pallas import tpu as pltpu

F32 = jnp.float32
BF16 = jnp.bfloat16

D_MODEL = 2048
DEPTH = 4
PAGE = 128
FOX_HEADS = 8
FOX_DIM = 128
FOX_WIDTH = FOX_HEADS * FOX_DIM
GLA_HEADS = 4
GLA_DK = 128
GLA_DV = 256
GLA_KEY_WIDTH = GLA_HEADS * GLA_DK
GLA_WIDTH = GLA_HEADS * GLA_DV
GLA_RANK = 16
GLA_NORMALIZER = 16.0
D_FF = 5632
CONV_W = 3
EPS = 1e-6
LOG2E = math.log2(math.e)

LANES = 128
SUBLANES = 8
BF16_ROWS = 16
MXU_COLS = 256
VMEM_LIMIT = 56 * 1024 * 1024

GLA_CHUNK = 128
GLA_SUB = 8
GLA_HEAD_BLOCK = 4
NEG_BIG = -1e30

TM = 1024
TN = 512
TQ = 512
SAMPLE_PAGES = 8
RING_SLOTS = 3

G_GQ, G_GK, G_GV, G_GR = 0, 512, 1024, 2048
G_WIDTH = 3072


def _params(sem):
    return pltpu.CompilerParams(dimension_semantics=sem, vmem_limit_bytes=VMEM_LIMIT)


def _log_sigmoid(x):
    return jnp.minimum(x, 0.0) - jnp.log(1.0 + jnp.exp(-jnp.abs(x)))


def _silu(x):
    return (0.5 * x) * (1.0 + jnp.tanh(0.5 * x))


def _rms(x, g):
    return x * lax.rsqrt(jnp.mean(x * x, axis=-1, keepdims=True) + EPS) * g


def _dot(a, b):
    return jnp.dot(a, b, preferred_element_type=F32)


def _dot_nt(a, b):
    return lax.dot_general(a, b, (((1,), (1,)), ((), ())), preferred_element_type=F32)


def _dot_tn(a, b):
    return lax.dot_general(a, b, (((0,), (0,)), ((), ())), preferred_element_type=F32)


def _split3(x):
    hi = x.astype(BF16).astype(F32)
    r1 = x - hi
    mid = r1.astype(BF16).astype(F32)
    lo = (r1 - mid).astype(BF16).astype(F32)
    return hi, mid, lo


def _norm_small_kernel(x_ref, g_ref, ws_ref, wa2_ref, ba_ref, bft_ref,
                       xn_ref, logf_ref, la_ref, c_ref, *rest, tm, seg, emit_cx):
    if emit_cx:
        cx_ref, carry_ref = rest
    else:
        (carry_ref,) = rest
    i = pl.program_id(0)
    xn = _rms(x_ref[...], g_ref[0]).astype(BF16)
    xn_ref[...] = xn
    zs = _dot_nt(xn, ws_ref[0])
    la_pre = _dot(zs.astype(BF16), wa2_ref[0]) + ba_ref[0]
    la_ref[...] = _log_sigmoid(la_pre) * (1.0 / GLA_NORMALIZER)

    pad_rows = -tm % LANES
    zs_sq = zs if pad_rows == 0 else jnp.concatenate([zs, jnp.zeros((pad_rows, LANES), F32)], axis=0)
    lft = _log_sigmoid(zs_sq.T[:FOX_HEADS, :tm] + bft_ref[0])
    logf_ref[0] = lft
    lane = lax.broadcasted_iota(jnp.int32, (FOX_HEADS, tm), 1)
    pos = lane & (seg - 1) if seg < tm else lane
    c = lft
    d = 1
    while d < min(seg, tm):
        c = c + jnp.where(pos >= d, pltpu.roll(c, d, axis=1), 0.0)
        d *= 2
    if seg > tm:
        tiles_per_seq = seg // tm

        @pl.when(i % tiles_per_seq == 0)
        def _():
            carry_ref[...] = jnp.zeros_like(carry_ref)

        c = c + carry_ref[:, 0:1]
        carry_ref[...] = jnp.broadcast_to(c[:, tm - 1:tm], carry_ref.shape)
    c_ref[0] = c

    if emit_cx:
        c2 = c * (-LOG2E)
        stacked = jnp.concatenate(
            [c2, c2, c2, jnp.zeros((LANES - 3 * FOX_HEADS, tm), F32)], axis=0)
        hi, mid, lo = _split3(stacked)
        rowid = lax.broadcasted_iota(jnp.int32, stacked.shape, 0)
        pieces = jnp.where(rowid < FOX_HEADS, hi, jnp.where(rowid < 2 * FOX_HEADS, mid, lo))
        cx_ref[...] = pieces.T.astype(BF16)


def norm_small(x, gamma, w_small, wa2_pad, b_a, bf_t, *, layer, tm, seg, emit_cx):
    m = x.shape[0]
    n_tiles = m // tm
    if seg >= tm:
        tps = seg // tm
        c_shape = (m // seg, FOX_HEADS, seg)
        c_map = lambda i: (i // tps, 0, i % tps)
    else:
        c_shape = (n_tiles, FOX_HEADS, tm)
        c_map = lambda i: (i, 0, 0)
    lay = lambda i: (layer, 0, 0)
    out_specs = [
        pl.BlockSpec((tm, D_MODEL), lambda i: (i, 0)),
        pl.BlockSpec((1, FOX_HEADS, tm), c_map),
        pl.BlockSpec((tm, GLA_KEY_WIDTH), lambda i: (i, 0)),
        pl.BlockSpec((1, FOX_HEADS, tm), c_map),
    ]
    out_shape = [
        jax.ShapeDtypeStruct((m, D_MODEL), BF16),
        jax.ShapeDtypeStruct(c_shape, F32),
        jax.ShapeDtypeStruct((m, GLA_KEY_WIDTH), F32),
        jax.ShapeDtypeStruct(c_shape, F32),
    ]
    if emit_cx:
        out_specs.append(pl.BlockSpec((tm, LANES), lambda i: (i, 0)))
        out_shape.append(jax.ShapeDtypeStruct((m, LANES), BF16))
    return pl.pallas_call(
        functools.partial(_norm_small_kernel, tm=tm, seg=seg, emit_cx=emit_cx),
        grid=(n_tiles,),
        in_specs=[
            pl.BlockSpec((tm, D_MODEL), lambda i: (i, 0)),
            pl.BlockSpec((1, 1, D_MODEL), lay),
            pl.BlockSpec((1, LANES, D_MODEL), lay),
            pl.BlockSpec((1, LANES, GLA_KEY_WIDTH), lay),
            pl.BlockSpec((1, 1, GLA_KEY_WIDTH), lay),
            pl.BlockSpec((1, FOX_HEADS, 1), lay),
        ],
        out_specs=out_specs,
        out_shape=out_shape,
        scratch_shapes=[pltpu.VMEM((FOX_HEADS, LANES), F32)],
        compiler_params=_params(("arbitrary",)),
        name="norm_small",
    )(x, gamma, w_small, wa2_pad, b_a, bf_t)


def _mm_kernel(*refs, n_a, w_t, has_scale, has_res, has_stack_in, n_out):
    a_refs = refs[:n_a]
    w_refs = refs[n_a:2 * n_a]
    pos = 2 * n_a
    mm = _dot_nt if w_t else _dot
    acc = mm(a_refs[0][...], w_refs[0][0].astype(BF16))
    for a_ref, w_ref in zip(a_refs[1:], w_refs[1:]):
        acc = acc + mm(a_ref[...], w_ref[0].astype(BF16))
    if has_scale:
        acc = acc * refs[pos][...]
        pos += 1
    if has_res:
        acc = refs[pos][...] + acc
        pos += 1
    if has_stack_in:
        pos += 1
    for o_ref in refs[pos:pos + n_out]:
        o_ref[...] = acc.astype(o_ref.dtype)


def matmul(a_list, w_list, *, layer, out_dtypes, tm, tn, n, w_rows=None, w_col0=0, w_t=False,
           scale=None, res=None, stacked=None):
    m = a_list[0].shape[0]
    n_i = m // tm
    w_rows = w_rows or [0] * len(a_list)
    in_specs = [pl.BlockSpec((tm, a.shape[1]), lambda i, j: (i, 0)) for a in a_list]
    for a, w, r in zip(a_list, w_list, w_rows):
        if w_t:
            in_specs.append(pl.BlockSpec((1, tn, a.shape[1]), lambda i, j, r=r: (layer, w_col0 + j, r)))
        else:
            in_specs.append(pl.BlockSpec((1, a.shape[1], tn), lambda i, j, r=r: (layer, r, w_col0 + j)))
    args = list(a_list) + list(w_list)
    if scale is not None:
        in_specs.append(pl.BlockSpec((1, tn), lambda i, j: (0, j)))
        args.append(scale)
    if res is not None:
        in_specs.append(pl.BlockSpec((tm, tn), lambda i, j: (i, j)))
        args.append(res)
    out_specs = [pl.BlockSpec((tm, tn), lambda i, j: (i, j)) for _ in out_dtypes]
    out_shape = [jax.ShapeDtypeStruct((m, n), dt) for dt in out_dtypes]
    aliases = {}
    has_stack_in = False
    if stacked is not None:
        out_specs[0] = pl.BlockSpec((tm, tn), lambda i, j: (layer * n_i + i, j))
        out_shape[0] = jax.ShapeDtypeStruct((DEPTH * m, n), out_dtypes[0])
        if stacked[0] is not None:
            has_stack_in = True
            in_specs.append(pl.BlockSpec(memory_space=pl.ANY))
            aliases = {len(args): 0}
            args.append(stacked[0])
    outs = pl.pallas_call(
        functools.partial(_mm_kernel, n_a=len(a_list), w_t=w_t, has_scale=scale is not None,
                          has_res=res is not None, has_stack_in=has_stack_in, n_out=len(out_dtypes)),
        grid=(n_i, n // tn),
        in_specs=in_specs,
        out_specs=out_specs,
        out_shape=out_shape,
        input_output_aliases=aliases,
        compiler_params=_params(("parallel", "arbitrary")),
        name="matmul",
    )(*args)
    return outs


def _fox_prompt_kernel(q_ref, k_ref, cx_ref, v_ref, gn_ref, o_ref, kx_sc, vt_sc, *, seq, tq):
    h = pl.program_id(1)
    kx_sc[:, 0:FOX_DIM] = k_ref[...]
    kx_sc[:, FOX_DIM:2 * FOX_DIM] = cx_ref[...]
    vt_sc[0:FOX_DIM, :] = v_ref[...].astype(F32).T.astype(BF16)
    vt_sc[FOX_DIM:FOX_DIM + BF16_ROWS, :] = jnp.ones((BF16_ROWS, seq), BF16)

    lane = lax.broadcasted_iota(jnp.int32, (tq, FOX_DIM), 1)
    pick = (lane == h) | (lane == h + FOX_HEADS) | (lane == h + 2 * FOX_HEADS)
    onehot = jnp.where(pick, 1.0, 0.0).astype(BF16)
    key_i = lax.broadcasted_iota(jnp.int32, (tq, tq), 0)
    qry_i = lax.broadcasted_iota(jnp.int32, (tq, tq), 1)
    causal = key_i <= qry_i

    def scores(qi):
        lo, hi = qi * tq, (qi + 1) * tq
        qx = jnp.concatenate([q_ref[lo:hi, :], onehot], axis=1)
        s_diag = jnp.where(causal, _dot_nt(kx_sc[lo:hi, :], qx), NEG_BIG)
        s_off = _dot_nt(kx_sc[0:lo, :], qx) if qi > 0 else None
        return s_diag, s_off

    n_q = seq // tq
    ahead = scores(0)
    for qi in range(n_q):
        lo, hi = qi * tq, (qi + 1) * tq
        s_diag, s_off = ahead
        if qi + 1 < n_q:
            ahead = scores(qi + 1)
        m = jnp.max(s_diag, axis=0, keepdims=True)
        if qi > 0:
            m = jnp.maximum(m, jnp.max(s_off, axis=0, keepdims=True))
            p_off = jnp.exp2(s_off - m).astype(BF16)
            ot = _dot(vt_sc[:, 0:lo], p_off)
        p_diag = jnp.exp2(s_diag - m).astype(BF16)
        ot_d = _dot(vt_sc[:, lo:hi], p_diag)
        ot = ot + ot_d if qi > 0 else ot_d
        o = ot[0:FOX_DIM, :] * (1.0 / ot[FOX_DIM:FOX_DIM + 1, :])
        y = o * lax.rsqrt(jnp.mean(o * o, axis=0, keepdims=True) + EPS)
        o_ref[lo:hi, :] = (y.T * gn_ref[0]).astype(o_ref.dtype)


def fox_prompt(q, k16, cx, v16, fox_norm, *, layer, batch, seq):
    m = batch * seq
    return pl.pallas_call(
        functools.partial(_fox_prompt_kernel, seq=seq, tq=TQ),
        grid=(batch, FOX_HEADS),
        in_specs=[
            pl.BlockSpec((seq, FOX_DIM), lambda b, h: (b, h)),
            pl.BlockSpec((seq, FOX_DIM), lambda b, h: (b, h)),
            pl.BlockSpec((seq, LANES), lambda b, h: (b, 0)),
            pl.BlockSpec((seq, FOX_DIM), lambda b, h: (b, h)),
            pl.BlockSpec((1, 1, FOX_DIM), lambda b, h: (layer, 0, h)),
        ],
        out_specs=pl.BlockSpec((seq, FOX_DIM), lambda b, h: (b, h)),
        out_shape=jax.ShapeDtypeStruct((m, FOX_WIDTH), BF16),
        scratch_shapes=[pltpu.VMEM((seq, 2 * FOX_DIM), BF16),
                        pltpu.VMEM((FOX_DIM + BF16_ROWS, seq), BF16)],
        compiler_params=_params(("parallel", "arbitrary")),
        name="fox_prompt",
    )(q, k16, cx, v16, fox_norm)


def _fox_sample_kernel(pt_ref, q_ref, kn_ref, vn_ref, cn_ref, gn_ref, k_hbm, v_hbm, *rest,
                       pages, t_new, layer, n_pages):
    lf_refs = rest[:pages]
    o_ref, m_sc, l_sc, acc_sc, carry_sc, k_buf, v_buf, sem = rest[pages:]
    step = pl.program_id(1)
    n_steps = pl.num_programs(1)
    rows = t_new

    t = pl.program_id(0) * n_steps + step
    total = pl.num_programs(0) * n_steps

    def page_copies(tt, slot):
        b_t = tt // n_steps
        first = b_t * n_pages + (n_steps - 1 - (tt - b_t * n_steps)) * pages
        copies = []
        for p in range(pages):
            page = pt_ref[first + p]
            copies.append(pltpu.make_async_copy(k_hbm.at[layer, page], k_buf.at[slot, p], sem.at[slot]))
            copies.append(pltpu.make_async_copy(v_hbm.at[layer, page], v_buf.at[slot, p], sem.at[slot]))
        return copies

    @pl.when(t == 0)
    def _():
        for tt in range(RING_SLOTS - 1):
            for c in page_copies(tt, tt):
                c.start()

    @pl.when(t + (RING_SLOTS - 1) < total)
    def _():
        ahead = t + (RING_SLOTS - 1)
        for c in page_copies(ahead, ahead % RING_SLOTS):
            c.start()

    slot = t % RING_SLOTS
    for c in page_copies(t, slot):
        c.wait()
    k_refs = [k_buf.at[slot, p] for p in range(pages)]
    v_refs = [v_buf.at[slot, p] for p in range(pages)]

    @pl.when(step == 0)
    def _():
        m_sc[...] = jnp.full_like(m_sc, NEG_BIG)
        l_sc[...] = jnp.zeros_like(l_sc)
        acc_sc[...] = jnp.zeros_like(acc_sc)
        carry_sc[...] = jnp.zeros_like(carry_sc)

    flat = PAGE * FOX_HEADS
    lane = lax.broadcasted_iota(jnp.int32, (pages, flat), 1)
    lf = jnp.concatenate([lf_refs[p][0, 0] for p in range(pages)], axis=0)
    suf, tot = lf, lf
    d = FOX_HEADS
    while d < flat:
        suf = suf + jnp.where(lane < flat - d, pltpu.roll(suf, flat - d, axis=1), 0.0)
        tot = tot + pltpu.roll(tot, d, axis=1)
        d *= 2
    run = carry_sc[0:1, :]
    bias = [None] * pages
    for p in range(pages - 1, -1, -1):
        bias[p] = ((suf[p:p + 1, :] - lf[p:p + 1, :]) + run) * LOG2E
        run = run + tot[p:p + 1, :]
    carry_sc[0:1, :] = run

    n_rows = FOX_HEADS * rows
    q_flat = jnp.concatenate(
        [q_ref[:, h * FOX_DIM:(h + 1) * FOX_DIM] for h in range(FOX_HEADS)], axis=0).astype(BF16)
    zeros = jnp.zeros_like(q_flat)
    q_pair = jnp.concatenate([jnp.concatenate([q_flat, zeros], axis=1),
                              jnp.concatenate([zeros, q_flat], axis=1)], axis=0)
    row_i = lax.broadcasted_iota(jnp.int32, (n_rows, flat), 0)
    col_i = lax.broadcasted_iota(jnp.int32, (n_rows, flat), 1)
    own = (row_i >> (rows.bit_length() - 1)) == (col_i & (FOX_HEADS - 1))

    def local_softmax(s, v_all):
        m = jnp.max(s, axis=-1, keepdims=True)
        pr = jnp.exp2(s - m)
        return m, jnp.sum(pr, axis=-1, keepdims=True), _dot(pr.astype(BF16), v_all)

    def merge(groups):
        m_old = m_sc[...]
        m_new = m_old
        for m, _, _ in groups:
            m_new = jnp.maximum(m_new, m)
        alpha = jnp.exp2(m_old - m_new)
        l_new = alpha * l_sc[...]
        acc = alpha * acc_sc[...]
        for m, l, o in groups:
            w = jnp.exp2(m - m_new)
            l_new = l_new + w * l
            acc = acc + w * o
        m_sc[...] = m_new
        l_sc[...] = l_new
        acc_sc[...] = acc

    s_parts = [None] * pages
    for a in range(0, pages, 2):
        kk = jnp.concatenate([k_refs[a][...], k_refs[a + 1][...]], axis=1).astype(BF16)
        s2 = _dot_nt(q_pair, kk)
        s_parts[a] = jnp.where(own, s2[0:n_rows] + bias[a], NEG_BIG)
        s_parts[a + 1] = jnp.where(own, s2[n_rows:2 * n_rows] + bias[a + 1], NEG_BIG)
    half = pages // 2
    groups = []
    for lo_p, hi_p in ((0, half), (half, pages)):
        groups.append(local_softmax(
            jnp.concatenate(s_parts[lo_p:hi_p], axis=1),
            jnp.concatenate([v_refs[p][...].astype(BF16) for p in range(lo_p, hi_p)], axis=0)))
    merge(groups)

    @pl.when(step == n_steps - 1)
    def _():
        valid = own & ((col_i >> (FOX_HEADS.bit_length() - 1)) <= (row_i & (rows - 1)))
        s_new = _dot_nt(q_flat, kn_ref[...]) - cn_ref[0] * LOG2E
        merge([local_softmax(jnp.where(valid, s_new, NEG_BIG), vn_ref[...])])
        for h in range(FOX_HEADS):
            cols = slice(h * FOX_DIM, (h + 1) * FOX_DIM)
            sl = slice(h * rows, (h + 1) * rows)
            o = acc_sc[sl, :] / l_sc[sl, :]
            o_ref[:, cols] = _rms(o, gn_ref[0, :, cols])


def fox_sample(page_table, q, k_new_page, v_new_page, c_new_flat, fox_norm, cache_k4, cache_v4,
               cache_lf_flat, *, layer, pages):
    db, n_pages = page_table.shape
    t_new = q.shape[0] // db
    assert t_new == SUBLANES and pages % 2 == 0 and n_pages % pages == 0
    n_steps = n_pages // pages
    flat = PAGE * FOX_HEADS
    pt_flat = page_table.reshape(-1)

    def page_map(p):
        def index(b, s, pt):
            return (layer, pt[b * n_pages + (n_steps - 1 - s) * pages + p], 0, 0)
        return index

    in_specs = [
        pl.BlockSpec((t_new, FOX_WIDTH), lambda b, s, pt: (b, 0)),
        pl.BlockSpec((flat, FOX_DIM), lambda b, s, pt: (b, 0)),
        pl.BlockSpec((flat, FOX_DIM), lambda b, s, pt: (b, 0)),
        pl.BlockSpec((1, 1, flat), lambda b, s, pt: (b, 0, 0)),
        pl.BlockSpec((1, 1, FOX_WIDTH), lambda b, s, pt: (layer, 0, 0)),
    ]
    in_specs += [pl.BlockSpec(memory_space=pl.ANY), pl.BlockSpec(memory_space=pl.ANY)]
    in_specs += [pl.BlockSpec((1, 1, 1, flat), page_map(p)) for p in range(pages)]
    grid_spec = pltpu.PrefetchScalarGridSpec(
        num_scalar_prefetch=1,
        grid=(db, n_steps),
        in_specs=in_specs,
        out_specs=pl.BlockSpec((t_new, FOX_WIDTH), lambda b, s, pt: (b, 0)),
        scratch_shapes=[pltpu.VMEM((FOX_HEADS * t_new, 1), F32),
                        pltpu.VMEM((FOX_HEADS * t_new, 1), F32),
                        pltpu.VMEM((FOX_HEADS * t_new, FOX_DIM), F32),
                        pltpu.VMEM((SUBLANES, flat), F32),
                        pltpu.VMEM((RING_SLOTS, pages, flat, FOX_DIM), F32),
                        pltpu.VMEM((RING_SLOTS, pages, flat, FOX_DIM), F32),
                        pltpu.SemaphoreType.DMA((RING_SLOTS,))],
    )
    assert db * n_steps >= RING_SLOTS - 1
    return pl.pallas_call(
        functools.partial(_fox_sample_kernel, pages=pages, t_new=t_new, layer=layer, n_pages=n_pages),
        grid_spec=grid_spec,
        out_shape=jax.ShapeDtypeStruct((db * t_new, FOX_WIDTH), F32),
        compiler_params=_params(("arbitrary", "arbitrary")),
        name="fox_sample",
    )(pt_flat, q, k_new_page, v_new_page, c_new_flat, fox_norm, cache_k4, cache_v4,
      *([cache_lf_flat] * pages))


def _gla_kernel(q_ref, k_ref, v_ref, la_ref, r_ref, gn_ref, s0_ref, o_ref, sfin_ref,
                s_sc, b_sc, *, chunk, sub, hb):
    ci = pl.program_id(2)
    n_chunks = pl.num_programs(2)
    heads = range(hb)
    kcols = lambda h: slice(h * GLA_DK, (h + 1) * GLA_DK)
    vcols = lambda h: slice(h * GLA_DV, (h + 1) * GLA_DV)

    @pl.when(ci == 0)
    def _():
        s_sc[...] = s0_ref[0]

    q = [q_ref[:, kcols(h)].astype(F32) for h in heads]
    k = [k_ref[:, kcols(h)].astype(F32) for h in heads]
    v = [v_ref[:, vcols(h)] for h in heads]
    rowi = lax.broadcasted_iota(jnp.int32, (chunk, chunk), 0)
    coli = lax.broadcasted_iota(jnp.int32, (chunk, chunk), 1)

    tri = jnp.where(rowi >= coli, 1.0, 0.0).astype(BF16)
    hi, mid, lo = _split3(la_ref[...])
    b_all = _dot(tri, hi.astype(BF16)) + _dot(tri, mid.astype(BF16)) + _dot(tri, lo.astype(BF16))
    b_sc[...] = b_all
    b = [b_all[:, kcols(h)] for h in heads]
    b_last = [b_sc[chunk - 1:chunk, kcols(h)] for h in heads]

    ones = jnp.ones((GLA_DK, chunk), BF16)
    diff = rowi - coli
    sub_shift = sub.bit_length() - 1
    a = [jnp.zeros((chunk, chunk), F32) for _ in heads]
    for d in range(sub):
        for h in heads:
            k_s = k[h] if d == 0 else pltpu.roll(k[h], d, axis=0)
            b_s = b[h] if d == 0 else pltpu.roll(b[h], d, axis=0)
            prod = q[h] * k_s * jnp.exp(b[h] - b_s)
            band = _dot(prod.astype(BF16), ones)
            a[h] = jnp.where(diff == d, band, a[h])
    in_sub = (rowi >> sub_shift) == (coli >> sub_shift)
    a = [jnp.where(in_sub, a[h], 0.0) for h in heads]

    m = sub
    while m < chunk:
        groups = chunk // (2 * m)
        shift = (2 * m).bit_length() - 1
        keep = ((rowi >> shift) == (coli >> shift)) & ((rowi & m) != 0) & ((coli & m) == 0)
        for h in heads:
            ref_rows = [jnp.broadcast_to(b_sc[g * 2 * m + m - 1:g * 2 * m + m, kcols(h)], (2 * m, GLA_DK))
                        for g in range(groups)]
            ref_b = ref_rows[0] if groups == 1 else jnp.concatenate(ref_rows, axis=0)
            e = jnp.exp(-jnp.abs(b[h] - ref_b))
            qm = (q[h] * e).astype(BF16)
            km = (k[h] * e).astype(BF16)
            a[h] = a[h] + jnp.where(keep, _dot_nt(qm, km), 0.0)
        m *= 2

    s_old = [s_sc[h] for h in heads]
    o = [_dot((q[h] * jnp.exp(b[h])).astype(BF16), s_old[h].astype(BF16))
         + _dot(a[h].astype(BF16), v[h]) for h in heads]

    s_new = []
    for h in heads:
        kl = (k[h] * jnp.exp(b_last[h] - b[h])).astype(BF16)
        decay = jnp.broadcast_to(jnp.exp(b_last[h]), (GLA_DK, GLA_DK)).T
        decay = jnp.concatenate([decay, decay], axis=1)
        s_new.append(decay * s_old[h] + _dot_tn(kl, v[h]))
        s_sc[h] = s_new[h]

    @pl.when(ci == n_chunks - 1)
    def _():
        for h in heads:
            sfin_ref[0, h] = s_new[h]

    for h in heads:
        r = r_ref[:, vcols(h)].astype(F32)
        o_ref[:, vcols(h)] = (_rms(o[h], gn_ref[0, :, vcols(h)]) * _silu(r)).astype(o_ref.dtype)


def gla(gp, la, gla_norm, s0, *, layer, batch, seq):
    m = batch * seq
    chunk = GLA_CHUNK
    hb = GLA_HEAD_BLOCK
    kw, vw = hb * GLA_DK, hb * GLA_DV
    n_chunks = seq // chunk
    rows = lambda b, h, c: b * n_chunks + c
    return pl.pallas_call(
        functools.partial(_gla_kernel, chunk=chunk, sub=GLA_SUB, hb=hb),
        grid=(batch, GLA_HEADS // hb, n_chunks),
        in_specs=[
            pl.BlockSpec((chunk, kw), lambda b, h, c: (rows(b, h, c), G_GQ // kw + h)),
            pl.BlockSpec((chunk, kw), lambda b, h, c: (rows(b, h, c), G_GK // kw + h)),
            pl.BlockSpec((chunk, vw), lambda b, h, c: (rows(b, h, c), G_GV // vw + h)),
            pl.BlockSpec((chunk, kw), lambda b, h, c: (rows(b, h, c), h)),
            pl.BlockSpec((chunk, vw), lambda b, h, c: (rows(b, h, c), G_GR // vw + h)),
            pl.BlockSpec((1, 1, vw), lambda b, h, c: (layer, 0, h)),
            pl.BlockSpec((1, hb, GLA_DK, GLA_DV), lambda b, h, c: (b, h, 0, 0)),
        ],
        out_specs=[
            pl.BlockSpec((chunk, vw), lambda b, h, c: (rows(b, h, c), h)),
            pl.BlockSpec((1, hb, GLA_DK, GLA_DV), lambda b, h, c: (b, h, 0, 0)),
        ],
        out_shape=[
            jax.ShapeDtypeStruct((m, GLA_WIDTH), BF16),
            jax.ShapeDtypeStruct((batch, GLA_HEADS, GLA_DK, GLA_DV), F32),
        ],
        scratch_shapes=[pltpu.VMEM((hb, GLA_DK, GLA_DV), F32), pltpu.VMEM((chunk, kw), F32)],
        compiler_params=_params(("parallel", "parallel", "arbitrary")),
        name="gla",
    )(gp, gp, gp, la, gp, gla_norm, s0)


def _conv_gate(g, g1, g2, u, cw_ref, cb_ref, cols):
    conv = (cb_ref[0, :, cols] + cw_ref[0, 0:1, cols] * g2 + cw_ref[0, 1:2, cols] * g1
            + cw_ref[0, 2:3, cols] * g)
    return _silu(conv) * u


def _ffn_up_prompt_kernel(x_ref, halo_ref, g_ref, wg_ref, wu_ref, cw_ref, cb_ref,
                          h_ref, gtail_ref, xn_sc, *, tm, tn, tiles_per_seq):
    i = pl.program_id(0)
    j = pl.program_id(1)

    @pl.when(j == 0)
    def _():
        xn_sc[0:tm, :] = _rms(x_ref[...], g_ref[0]).astype(BF16)
        keep = jnp.where(i % tiles_per_seq == 0, 0.0, 1.0)
        xn_sc[tm:tm + BF16_ROWS, :] = (_rms(halo_ref[...], g_ref[0]) * keep).astype(BF16)

    for t in range(tn // MXU_COLS):
        cols = slice(t * MXU_COLS, (t + 1) * MXU_COLS)
        g_ext = _dot(xn_sc[...], wg_ref[0, :, cols].astype(BF16))
        u = _dot(xn_sc[0:tm, :], wu_ref[0, :, cols].astype(BF16))
        g = g_ext[0:tm]
        g1 = pltpu.roll(g_ext, 1, axis=0)[0:tm]
        g2 = pltpu.roll(g_ext, 2, axis=0)[0:tm]
        h_ref[:, cols] = _conv_gate(g, g1, g2, u, cw_ref, cb_ref, cols).astype(h_ref.dtype)
        gtail_ref[:, cols] = g[tm - SUBLANES:tm]


def ffn_up_prompt(x, gamma, wg, wu, conv_w, conv_b, *, layer, seq, tm, tn):
    m = x.shape[0]
    tiles_per_seq = seq // tm
    halo_blocks = tm // BF16_ROWS
    return pl.pallas_call(
        functools.partial(_ffn_up_prompt_kernel, tm=tm, tn=tn, tiles_per_seq=tiles_per_seq),
        grid=(m // tm, D_FF // tn),
        in_specs=[
            pl.BlockSpec((tm, D_MODEL), lambda i, j: (i, 0)),
            pl.BlockSpec((BF16_ROWS, D_MODEL), lambda i, j: (jnp.maximum(i * halo_blocks - 1, 0), 0)),
            pl.BlockSpec((1, 1, D_MODEL), lambda i, j: (layer, 0, 0)),
            pl.BlockSpec((1, D_MODEL, tn), lambda i, j: (layer, 0, j)),
            pl.BlockSpec((1, D_MODEL, tn), lambda i, j: (layer, 0, j)),
            pl.BlockSpec((1, CONV_W, tn), lambda i, j: (layer, 0, j)),
            pl.BlockSpec((1, 1, tn), lambda i, j: (layer, 0, j)),
        ],
        out_specs=[
            pl.BlockSpec((tm, tn), lambda i, j: (i, j)),
            pl.BlockSpec((SUBLANES, tn), lambda i, j: (i, j)),
        ],
        out_shape=[
            jax.ShapeDtypeStruct((m, D_FF), BF16),
            jax.ShapeDtypeStruct((m // tm * SUBLANES, D_FF), F32),
        ],
        scratch_shapes=[pltpu.VMEM((tm + BF16_ROWS, D_MODEL), BF16)],
        compiler_params=_params(("parallel", "arbitrary")),
        name="ffn_up_prompt",
    )(x, x, gamma, wg, wu, conv_w, conv_b)


def _ffn_up_sample_kernel(x_ref, g_ref, wg_ref, wu_ref, cw_ref, cb_ref, e1_ref, e2_ref,
                          h_ref, gout_ref, xn_sc, *, seq, tn):
    j = pl.program_id(0)

    @pl.when(j == 0)
    def _():
        xn_sc[...] = _rms(x_ref[...], g_ref[0]).astype(BF16)

    cols = slice(0, tn)
    g = _dot(xn_sc[...], wg_ref[0].astype(BF16))
    u = _dot(xn_sc[...], wu_ref[0].astype(BF16))
    pos = lax.broadcasted_iota(jnp.int32, (g.shape[0], 1), 0) & (seq - 1)
    g1 = jnp.where(pos >= 1, pltpu.roll(g, 1, axis=0), e1_ref[...])
    g2 = jnp.where(pos >= 2, pltpu.roll(g, 2, axis=0), e2_ref[...])
    h_ref[...] = _conv_gate(g, g1, g2, u, cw_ref, cb_ref, cols).astype(h_ref.dtype)
    gout_ref[...] = g


def ffn_up_sample(x, gamma, wg, wu, conv_w, conv_b, e1, e2, *, layer, seq, tn):
    m = x.shape[0]
    return pl.pallas_call(
        functools.partial(_ffn_up_sample_kernel, seq=seq, tn=tn),
        grid=(D_FF // tn,),
        in_specs=[
            pl.BlockSpec((m, D_MODEL), lambda j: (0, 0)),
            pl.BlockSpec((1, 1, D_MODEL), lambda j: (layer, 0, 0)),
            pl.BlockSpec((1, D_MODEL, tn), lambda j: (layer, 0, j)),
            pl.BlockSpec((1, D_MODEL, tn), lambda j: (layer, 0, j)),
            pl.BlockSpec((1, CONV_W, tn), lambda j: (layer, 0, j)),
            pl.BlockSpec((1, 1, tn), lambda j: (layer, 0, j)),
            pl.BlockSpec((m, tn), lambda j: (0, j)),
            pl.BlockSpec((m, tn), lambda j: (0, j)),
        ],
        out_specs=[
            pl.BlockSpec((m, tn), lambda j: (0, j)),
            pl.BlockSpec((m, tn), lambda j: (0, j)),
        ],
        out_shape=[
            jax.ShapeDtypeStruct((m, D_FF), BF16),
            jax.ShapeDtypeStruct((m, D_FF), F32),
        ],
        scratch_shapes=[pltpu.VMEM((m, D_MODEL), BF16)],
        compiler_params=_params(("arbitrary",)),
        name="ffn_up_sample",
    )(x, gamma, wg, wu, conv_w, conv_b, e1, e2)


def _rmsnorm_kernel(x_ref, g_ref, o_ref):
    o_ref[...] = _rms(x_ref[...], g_ref[...])


def rmsnorm(x, gamma, *, tm):
    m = x.shape[0]
    return pl.pallas_call(
        _rmsnorm_kernel,
        grid=(m // tm,),
        in_specs=[pl.BlockSpec((tm, D_MODEL), lambda i: (i, 0)),
                  pl.BlockSpec((1, D_MODEL), lambda i: (0, 0))],
        out_specs=pl.BlockSpec((tm, D_MODEL), lambda i: (i, 0)),
        out_shape=jax.ShapeDtypeStruct((m, D_MODEL), F32),
        compiler_params=_params(("parallel",)),
        name="rmsnorm",
    )(x, gamma)


def _window_cast_kernel(a_ref, b_ref, o_ref, *, shift):
    x = jnp.concatenate([a_ref[0], b_ref[0]], axis=0)
    o_ref[0] = x[shift:, :].astype(o_ref.dtype)


def window_cast(w_t, *, row0, width, rows):
    depth, _, k_dim = w_t.shape
    blk, shift = divmod(row0, rows)
    assert shift == SUBLANES and width % rows == 0
    return pl.pallas_call(
        functools.partial(_window_cast_kernel, shift=shift),
        grid=(depth, width // rows),
        in_specs=[pl.BlockSpec((1, rows, k_dim), lambda l, i: (l, blk + i, 0)),
                  pl.BlockSpec((1, shift, k_dim), lambda l, i: (l, (blk + i + 1) * (rows // shift), 0))],
        out_specs=pl.BlockSpec((1, rows, k_dim), lambda l, i: (l, i, 0)),
        out_shape=jax.ShapeDtypeStruct((depth, width, k_dim), BF16),
        compiler_params=_params(("parallel", "parallel")),
        name="window_cast",
    )(w_t, w_t)


def _prep_weights(w_in_t, b_f, w_a2, b_a, w_out, w_gate, w_up, w_down):
    o_fl = 3 * FOX_WIDTH
    o_gq = o_fl + FOX_HEADS
    o_ga = o_gq + 2 * GLA_KEY_WIDTH + 2 * GLA_WIDTH
    w_fl, w_ga = lax.optimization_barrier(
        (w_in_t[:, o_fl:o_fl + FOX_HEADS, :], w_in_t[:, o_ga:o_ga + GLA_RANK, :]))
    pad = LANES - FOX_HEADS - GLA_RANK
    zeros = lambda *s: jnp.zeros((DEPTH,) + s, F32)
    return dict(
        w_g=window_cast(w_in_t, row0=o_gq, width=G_WIDTH, rows=256),
        w_small=jnp.concatenate([w_fl, w_ga, zeros(pad, D_MODEL)], axis=1).astype(BF16),
        wa2_pad=jnp.concatenate([zeros(FOX_HEADS, GLA_KEY_WIDTH), w_a2, zeros(pad, GLA_KEY_WIDTH)],
                                axis=1).astype(BF16),
        b_a=b_a.reshape(DEPTH, 1, GLA_KEY_WIDTH),
        bf_t=b_f.reshape(DEPTH, FOX_HEADS, 1),
        w_out=w_out.astype(BF16), w_gate=w_gate, w_up=w_up,
        w_down=w_down.astype(BF16),
    )


def _pad_rows(x, batch, t, t_pad):
    w = x.shape[-1]
    return jnp.pad(x.reshape(batch, t, w), ((0, 0), (0, t_pad - t), (0, 0))).reshape(batch * t_pad, w)


def kernel(x_prompt, x_sample, cache_k, cache_v, cache_logf, state_gla, state_conv, page_table,
           w_in, b_f, w_a2, b_a, fox_norm, gla_norm, w_out, norm_attn, norm_ffn,
           w_gate, w_up, conv_w, conv_b, w_down, norm_final):
    bp, seq, _ = x_prompt.shape
    db, t_new, _ = x_sample.shape
    n_pool = cache_k.shape[1]
    mp, ms = bp * seq, db * t_new

    xp = x_prompt.reshape(mp, D_MODEL)
    xs = x_sample.reshape(ms, D_MODEL)
    cache_k4 = cache_k.reshape(DEPTH, n_pool, PAGE * FOX_HEADS, FOX_DIM)
    cache_v4 = cache_v.reshape(DEPTH, n_pool, PAGE * FOX_HEADS, FOX_DIM)
    cache_lf_flat = cache_logf.reshape(DEPTH, n_pool, 1, PAGE * FOX_HEADS)
    q_scale = jnp.full((1, FOX_WIDTH), FOX_DIM ** -0.5 * LOG2E, F32)
    g_scale = jnp.concatenate([jnp.full((GLA_KEY_WIDTH,), GLA_DK ** -0.5, F32),
                               jnp.ones((G_WIDTH - GLA_KEY_WIDTH,), F32)]).reshape(1, G_WIDTH)
    zero_state = jnp.zeros((bp, GLA_HEADS, GLA_DK, GLA_DV), F32)
    w_in_t = jnp.swapaxes(w_in, 1, 2)
    w = _prep_weights(w_in_t, b_f, w_a2, b_a, w_out, w_gate, w_up, w_down)
    conv_b3 = conv_b.reshape(DEPTH, 1, D_FF)
    norm_attn = norm_attn.reshape(DEPTH, 1, D_MODEL)
    norm_ffn = norm_ffn.reshape(DEPTH, 1, D_MODEL)
    fox_norm = fox_norm.reshape(DEPTH, 1, FOX_WIDTH)
    gla_norm = gla_norm.reshape(DEPTH, 1, GLA_WIDTH)
    small = (w["w_small"], w["wa2_pad"], w["b_a"], w["bf_t"])

    def projections(xn, l, tm, k_stack, v_stack):
        wide = 2 * TN
        (q,) = matmul([xn], [w_in_t], layer=l, out_dtypes=[BF16], tm=tm, tn=wide, n=FOX_WIDTH, w_col0=0,
                      w_t=True, scale=q_scale)
        k_stack, k16 = matmul([xn], [w_in_t], layer=l, out_dtypes=[F32, BF16], tm=tm, tn=wide, n=FOX_WIDTH,
                              w_col0=1, w_t=True, stacked=(k_stack,))
        v_stack, v16 = matmul([xn], [w_in_t], layer=l, out_dtypes=[F32, BF16], tm=tm, tn=wide, n=FOX_WIDTH,
                              w_col0=2, w_t=True, stacked=(v_stack,))
        (gp,) = matmul([xn], [w["w_g"]], layer=l, out_dtypes=[BF16], tm=tm, tn=wide, n=G_WIDTH, w_t=True,
                       scale=g_scale)
        return q, k_stack, k16, v_stack, v16, gp

    def out_and_down(fo, go, h_fn, x, l, tm_out, tm):
        (x,) = matmul([fo, go], [w["w_out"], w["w_out"]], layer=l, out_dtypes=[F32], tm=tm_out, tn=D_MODEL,
                      n=D_MODEL, w_rows=[0, 1], res=x)
        h, extra = h_fn(x)
        (x,) = matmul([h], [w["w_down"]], layer=l, out_dtypes=[F32], tm=tm, tn=TN, n=D_MODEL, res=x)
        return x, extra

    outs = {k: [] for k in ("lp", "sp", "cp", "ls", "ss", "cs")}
    kp = jnp.zeros((DEPTH * mp, FOX_WIDTH), F32)
    vp = jnp.zeros((DEPTH * mp, FOX_WIDTH), F32)
    ksm = jnp.zeros((DEPTH * ms, FOX_WIDTH), F32)
    vsm = jnp.zeros((DEPTH * ms, FOX_WIDTH), F32)
    for l in range(DEPTH):
        xn, logf, la, _, cx = norm_small(xp, norm_attn, *small, layer=l, tm=TM, seg=seq, emit_cx=True)
        q, kp, k16, vp, v16, gp = projections(xn, l, TM, kp, vp)
        fo = fox_prompt(q, k16, cx, v16, fox_norm, layer=l, batch=bp, seq=seq)
        go, s_fin = gla(gp, la, gla_norm, zero_state, layer=l, batch=bp, seq=seq)
        ffn = lambda x: ffn_up_prompt(x, norm_ffn, w["w_gate"], w["w_up"], conv_w, conv_b3,
                                      layer=l, seq=seq, tm=TM, tn=TN)
        xp, gtail = out_and_down(fo, go, ffn, xp, l, TM // 2, TM)
        outs["lp"].append(logf.transpose(0, 2, 1))
        outs["sp"].append(s_fin)
        outs["cp"].append(gtail.reshape(bp, seq // TM, SUBLANES, D_FF)[:, -1, SUBLANES - (CONV_W - 1):, :])

        xn, logf, la, c = norm_small(xs, norm_attn, *small, layer=l, tm=ms, seg=t_new, emit_cx=False)
        q, ksm, k16, vsm, v16, gp = projections(xn, l, ms, ksm, vsm)
        kn_page = _pad_rows(k16, db, t_new, PAGE).reshape(db * PAGE * FOX_HEADS, FOX_DIM)
        vn_page = _pad_rows(v16, db, t_new, PAGE).reshape(db * PAGE * FOX_HEADS, FOX_DIM)
        c_new = c.reshape(FOX_HEADS, db, t_new).transpose(1, 2, 0)
        c_new_flat = jnp.pad(c_new, ((0, 0), (0, PAGE - t_new), (0, 0))).reshape(db, 1, PAGE * FOX_HEADS)
        fo = fox_sample(page_table, q.astype(F32), kn_page, vn_page, c_new_flat, fox_norm,
                        cache_k4, cache_v4, cache_lf_flat, layer=l, pages=SAMPLE_PAGES)
        gp_pad = _pad_rows(gp, db, t_new, GLA_CHUNK)
        la_pad = _pad_rows(la, db, t_new, GLA_CHUNK)
        go_pad, s_fin = gla(gp_pad, la_pad, gla_norm, state_gla[l], layer=l, batch=db, seq=GLA_CHUNK)
        go = go_pad.reshape(db, GLA_CHUNK, GLA_WIDTH)[:, :t_new].reshape(ms, GLA_WIDTH)
        st = state_conv[l]
        zeros_f = jnp.zeros((db, t_new - 1, D_FF), F32)
        e1 = jnp.concatenate([st[:, 1:2], zeros_f], axis=1).reshape(ms, D_FF)
        e2 = jnp.concatenate([st[:, 0:2], zeros_f[:, 1:]], axis=1).reshape(ms, D_FF)
        ffn = lambda x: ffn_up_sample(x, norm_ffn, w["w_gate"], w["w_up"], conv_w, conv_b3, e1, e2,
                                      layer=l, seq=t_new, tn=TN)
        xs, g_all = out_and_down(fo.astype(BF16), go, ffn, xs, l, ms, ms)
        outs["ls"].append(logf.reshape(FOX_HEADS, db, t_new).transpose(1, 2, 0))
        outs["ss"].append(s_fin)
        outs["cs"].append(g_all.reshape(db, t_new, D_FF)[:, t_new - (CONV_W - 1):, :])

    g_fin = norm_final.reshape(1, D_MODEL)
    y_prompt = rmsnorm(xp, g_fin, tm=TM).reshape(bp, seq, D_MODEL)
    y_sample = rmsnorm(xs, g_fin, tm=ms).reshape(db, t_new, D_MODEL)
    stack = lambda key: jnp.stack(outs[key])
    kv_p = lambda t: t.reshape(DEPTH, bp, seq, FOX_HEADS, FOX_DIM)
    kv_s = lambda t: t.reshape(DEPTH, db, t_new, FOX_HEADS, FOX_DIM)
    return (y_prompt, y_sample,
            kv_p(kp), kv_p(vp), stack("lp"), stack("sp"), stack("cp"),
            kv_s(ksm), kv_s(vsm), stack("ls"), stack("ss"), stack("cs"))
```
